```python
import math
import jax, jax.numpy as jnp
from jax import lax
import numpy as np

D_MODEL = 1024
BATCH = 8
SEQ = 4096
DEPTH = 4

DN_HEADS = 4
DN_DK = 128
DN_DV = 128
DN_CONV = 4
DN_CHUNK = 64
DN_QK = DN_HEADS * DN_DK
DN_VW = DN_HEADS * DN_DV
DN_QKV = 2 * DN_QK + DN_VW
CF_WIDTH = 512
CF_KERNEL = 31
S5_WIDTH = 512
S5_GROUP = 16
S5_GROUPS = S5_WIDTH // S5_GROUP
S5_STATE = 64
GLA_HEADS = 4
GLA_DK = 64
GLA_DV = 128
GLA_QK = GLA_HEADS * GLA_DK
GLA_VW = GLA_HEADS * GLA_DV
GLA_RANK = 16
GLA_TAU = 16.0
GLA_CHUNK = 16
N_BRANCH = 4
D_FF = 2816
FFN_CONV = 3
LN_EPS = 1e-5
DEEPNORM_ALPHA = (2.0 * DEPTH) ** 0.25
DEEPNORM_BETA = (8.0 * DEPTH) ** -0.25
IN_SIZES = (DN_QKV, DN_HEADS, DN_HEADS, DN_VW, 2 * CF_WIDTH, S5_WIDTH,
            GLA_QK, GLA_QK, GLA_VW, GLA_VW, GLA_RANK, N_BRANCH * D_MODEL)
IN_COLS = DN_QKV + 2 * DN_HEADS + DN_VW + 2 * CF_WIDTH + S5_WIDTH + 2 * GLA_QK + 2 * GLA_VW + GLA_RANK + N_BRANCH * D_MODEL

kernel_name = 'hybrid_deltanet_conformer_s5_gla_deepnorm'


def layer_norm(x, g, b):
    xf = x.astype(jnp.float32)
    mu = xf.mean(-1, keepdims=True)
    var = jnp.square(xf - mu).mean(-1, keepdims=True)
    return ((xf - mu) * lax.rsqrt(var + LN_EPS) * g.astype(jnp.float32) + b.astype(jnp.float32)).astype(x.dtype)


def rms_norm(x, g):
    xf = x.astype(jnp.float32)
    return (xf * lax.rsqrt(jnp.mean(xf * xf, -1, keepdims=True) + LN_EPS) * g.astype(jnp.float32)).astype(x.dtype)


def l2_normalize(x):
    xf = x.astype(jnp.float32)
    return (xf * lax.rsqrt(jnp.sum(xf * xf, -1, keepdims=True) + 1e-6)).astype(x.dtype)


def causal_dwconv(x, w):
    width, ch = w.shape
    return lax.conv_general_dilated(x, w[:, None, :].astype(x.dtype), window_strides=(1,),
                                    padding=[(width - 1, 0)],
                                    dimension_numbers=('NWC', 'WIO', 'NWC'),
                                    feature_group_count=ch)


def to_heads(t, h):
    b, l, c = t.shape
    return t.reshape(b, l, h, c // h).transpose(0, 2, 1, 3)


def from_heads(t):
    b, h, l, d = t.shape
    return t.transpose(0, 2, 1, 3).reshape(b, l, h * d)


def gated_delta_rule(q, k, v, g, beta):
    bsz, nh, seqlen, dk = q.shape
    dv = v.shape[-1]
    c = DN_CHUNK
    n = seqlen // c
    f32 = jnp.float32
    q, k, v = (t.astype(f32).reshape(bsz, nh, n, c, -1) for t in (q, k, v))
    g = g.astype(f32).reshape(bsz, nh, n, c)
    beta = beta.astype(f32).reshape(bsz, nh, n, c)
    gc = jnp.cumsum(g, axis=-1)
    causal = jnp.tril(jnp.ones((c, c), dtype=bool))
    strict = jnp.tril(jnp.ones((c, c), dtype=bool), k=-1)
    decay = jnp.where(causal, jnp.exp(jnp.where(causal, gc[..., :, None] - gc[..., None, :], 0.0)), 0.0)
    kb = k * beta[..., None]
    lower = jnp.where(strict, jnp.einsum('bhnid,bhnjd->bhnij', kb, k) * decay, 0.0)
    system = lower + jnp.eye(c, dtype=f32)
    rhs = jnp.concatenate([v * beta[..., None], kb * jnp.exp(gc)[..., None]], axis=-1)
    sol = lax.linalg.triangular_solve(system, rhs, left_side=True, lower=True, unit_diagonal=True)
    u, w = sol[..., :dv], sol[..., dv:]
    intra = jnp.einsum('bhnid,bhnjd->bhnij', q, k) * decay
    q_dec = q * jnp.exp(gc)[..., None]
    k_dec = k * jnp.exp(gc[..., -1:] - gc)[..., None]
    last = jnp.exp(gc[..., -1])

    def step(state, xs):
        u_c, w_c, a_c, q_c, k_c, l_c = xs
        v_new = u_c - jnp.einsum('bhcd,bhde->bhce', w_c, state)
        o_c = jnp.einsum('bhcd,bhde->bhce', q_c, state) + jnp.einsum('bhij,bhje->bhie', a_c, v_new)
        state = state * l_c[..., None, None] + jnp.einsum('bhcd,bhce->bhde', k_c, v_new)
        return state, o_c

    xs = tuple(jnp.moveaxis(t, 2, 0) for t in (u, w, intra, q_dec, k_dec, last))
    _, o = lax.scan(step, jnp.zeros((bsz, nh, dk, dv), f32), xs)
    return jnp.moveaxis(o, 0, 2).reshape(bsz, nh, seqlen, dv)


def gla_chunked(q, k, v, log_a):
    bsz, nh, seqlen, dk = q.shape
    dv = v.shape[-1]
    c = GLA_CHUNK
    n = seqlen // c
    f32 = jnp.float32
    q, k, v, log_a = (t.astype(f32).reshape(bsz, nh, n, c, -1) for t in (q, k, v, log_a))
    gc = jnp.cumsum(log_a, axis=3)
    q_dec = q * jnp.exp(gc)
    k_inv = k * jnp.exp(-gc)
    k_dec = k * jnp.exp(gc[..., -1:, :] - gc)
    last = jnp.exp(gc[..., -1, :])
    causal = jnp.tril(jnp.ones((c, c), dtype=bool))
    scores = jnp.where(causal, jnp.einsum('bhnid,bhnjd->bhnij', q_dec, k_inv), 0.0)
    intra = jnp.einsum('bhnij,bhnje->bhnie', scores, v)

    def step(state, xs):
        q_c, k_c, v_c, l_c = xs
        o_c = jnp.einsum('bhcd,bhde->bhce', q_c, state)
        state = state * l_c[..., :, None] + jnp.einsum('bhcd,bhce->bhde', k_c, v_c)
        return state, o_c

    xs = tuple(jnp.moveaxis(t, 2, 0) for t in (q_dec, k_dec, v, last))
    _, inter = lax.scan(step, jnp.zeros((bsz, nh, dk, dv), f32), xs)
    return (jnp.moveaxis(inter, 0, 2) + intra).reshape(bsz, nh, seqlen, dv)


def s5_ssm(u, a_re, a_im, log_dt, b_re, b_im, c_re, c_im, d):
    bsz, seqlen, _ = u.shape
    f32 = jnp.float32
    uf = u.astype(f32).reshape(bsz, seqlen, S5_GROUPS, S5_GROUP)
    dt = jnp.exp(log_dt.astype(f32))[:, None]
    ar, ai = a_re.astype(f32), a_im.astype(f32)
    mag = jnp.exp(dt * ar)
    abar_re, abar_im = mag * jnp.cos(dt * ai), mag * jnp.sin(dt * ai)
    den = ar * ar + ai * ai
    nr, ni = abar_re - 1.0, abar_im
    fr, fi = (nr * ar + ni * ai) / den, (ni * ar - nr * ai) / den
    br, bi = b_re.astype(f32), b_im.astype(f32)
    bbar_re = fr[..., None] * br - fi[..., None] * bi
    bbar_im = fr[..., None] * bi + fi[..., None] * br
    bu_re = jnp.einsum('blgh,gnh->blgn', uf, bbar_re)
    bu_im = jnp.einsum('blgh,gnh->blgn', uf, bbar_im)
    a_re_s = jnp.broadcast_to(abar_re, (1, seqlen, S5_GROUPS, S5_STATE))
    a_im_s = jnp.broadcast_to(abar_im, (1, seqlen, S5_GROUPS, S5_STATE))

    def combine(e1, e2):
        a1r, a1i, b1r, b1i = e1
        a2r, a2i, b2r, b2i = e2
        return (a2r * a1r - a2i * a1i, a2r * a1i + a2i * a1r,
                a2r * b1r - a2i * b1i + b2r, a2r * b1i + a2i * b1r + b2i)

    _, _, xr, xi = lax.associative_scan(combine, (a_re_s, a_im_s, bu_re, bu_im), axis=1)
    y = (jnp.einsum('blgn,ghn->blgh', xr, c_re.astype(f32))
         - jnp.einsum('blgn,ghn->blgh', xi, c_im.astype(f32))
         + d.astype(f32).reshape(S5_GROUPS, S5_GROUP) * uf)
    return y.reshape(bsz, seqlen, S5_WIDTH).astype(u.dtype)


def token_mixer(x, w_in, dn_conv, dn_a_log, dn_dt_bias, dn_norm, w_br_dn,
                cf_dw, cf_dw_bias, cf_ln_g, cf_ln_b, w_br_cf,
                s5_a_re, s5_a_im, s5_log_dt, s5_b_re, s5_b_im, s5_c_re, s5_c_im, s5_d, w_br_s5,
                gla_w_alpha, gla_b_alpha, gla_norm, w_br_gla, w_o):
    bsz, seqlen, _ = x.shape
    splits = [int(s) for s in np.cumsum(IN_SIZES)[:-1]]
    h = x @ w_in
    (dn_qkv, dn_a, dn_b, dn_gate, cf_in, s5_in,
     gla_q, gla_k, gla_v, gla_g, gla_lr, gate_logits) = jnp.split(h, splits, axis=-1)

    qkv = jax.nn.silu(causal_dwconv(dn_qkv, dn_conv))
    q, k, v = jnp.split(qkv, [DN_QK, 2 * DN_QK], axis=-1)
    q = l2_normalize(to_heads(q, DN_HEADS)) * (DN_DK ** -0.5)
    k = l2_normalize(to_heads(k, DN_HEADS))
    v = to_heads(v, DN_HEADS)
    g = -jnp.exp(dn_a_log.astype(jnp.float32)) * jax.nn.softplus(dn_a.astype(jnp.float32) + dn_dt_bias.astype(jnp.float32))
    beta = jax.nn.sigmoid(dn_b.astype(jnp.float32))
    o_dn = gated_delta_rule(q, k, v, g.transpose(0, 2, 1), beta.transpose(0, 2, 1)).astype(x.dtype)
    o_dn = rms_norm(o_dn, dn_norm) * jax.nn.silu(to_heads(dn_gate, DN_HEADS))
    y_a = from_heads(o_dn) @ w_br_dn

    cf_a, cf_g = jnp.split(cf_in, 2, axis=-1)
    c = causal_dwconv(cf_a * jax.nn.sigmoid(cf_g), cf_dw) + cf_dw_bias
    y_b = jax.nn.silu(layer_norm(c, cf_ln_g, cf_ln_b)) @ w_br_cf

    z = jax.nn.gelu(s5_ssm(s5_in, s5_a_re, s5_a_im, s5_log_dt, s5_b_re, s5_b_im, s5_c_re, s5_c_im, s5_d))
    z_val, z_gate = jnp.split(z @ w_br_s5, 2, axis=-1)
    y_c = z_val * jax.nn.sigmoid(z_gate)

    log_a = jax.nn.log_sigmoid((gla_lr @ gla_w_alpha + gla_b_alpha).astype(jnp.float32)) / GLA_TAU
    o_gla = gla_chunked(to_heads(gla_q, GLA_HEADS) * (GLA_DK ** -0.5), to_heads(gla_k, GLA_HEADS),
                        to_heads(gla_v, GLA_HEADS), to_heads(log_a, GLA_HEADS)).astype(x.dtype)
    o_gla = rms_norm(o_gla, gla_norm) * jax.nn.silu(to_heads(gla_g, GLA_HEADS))
    y_d = from_heads(o_gla) @ w_br_gla

    g_a, g_b, g_c, g_d = jnp.split(jax.nn.sigmoid(gate_logits), N_BRANCH, axis=-1)
    merged = g_a * y_a + g_b * y_b + g_c * y_c + g_d * y_d
    return merged @ w_o


def conv_ffn(x, w_up, ffn_conv, w_down):
    u = causal_dwconv(x @ w_up, ffn_conv)
    a, b = jnp.split(u, 2, axis=-1)
    return (jax.nn.silu(a) * b) @ w_down


def setup_inputs(seed: int = 0) -> dict:
    key = jax.random.key(seed)
    ks = jax.random.split(key, 40)
    f32 = jnp.float32
    L = DEPTH

    def nrm(k, shape, scale):
        return scale * jax.random.normal(k, shape, f32)

    def gain(k, shape):
        return 1.0 + 0.02 * jax.random.normal(k, shape, f32)

    dt_dn = jnp.exp(jax.random.uniform(ks[4], (L, DN_HEADS), f32, math.log(1e-3), math.log(1e-1)))
    n_idx = jnp.arange(S5_STATE, dtype=f32)
    return {
        'x': nrm(ks[0], (BATCH, SEQ, D_MODEL), 1.0),
        'w_in': nrm(ks[1], (L, D_MODEL, IN_COLS), D_MODEL ** -0.5),
        'dn_conv': nrm(ks[2], (L, DN_CONV, DN_QKV), DN_CONV ** -0.5),
        'dn_a_log': jnp.log(jax.random.uniform(ks[3], (L, DN_HEADS), f32, 1.0, 16.0)),
        'dn_dt_bias': dt_dn + jnp.log(-jnp.expm1(-dt_dn)),
        'dn_norm': gain(ks[5], (L, DN_DV)),
        'w_br_dn': nrm(ks[6], (L, DN_VW, D_MODEL), DN_VW ** -0.5),
        'cf_dw': nrm(ks[7], (L, CF_KERNEL, CF_WIDTH), CF_KERNEL ** -0.5),
        'cf_dw_bias': nrm(ks[8], (L, CF_WIDTH), 0.02),
        'cf_ln_g': gain(ks[9], (L, CF_WIDTH)),
        'cf_ln_b': nrm(ks[10], (L, CF_WIDTH), 0.02),
        'w_br_cf': nrm(ks[11], (L, CF_WIDTH, D_MODEL), CF_WIDTH ** -0.5),
        's5_a_re': -0.5 + nrm(ks[12], (L, S5_GROUPS, S5_STATE), 0.01),
        's5_a_im': math.pi * n_idx + nrm(ks[13], (L, S5_GROUPS, S5_STATE), 0.01),
        's5_log_dt': jax.random.uniform(ks[14], (L, S5_GROUPS), f32, math.log(1e-3), math.log(1e-1)),
        's5_b_re': nrm(ks[15], (L, S5_GROUPS, S5_STATE, S5_GROUP), (2.0 * S5_GROUP) ** -0.5),
        's5_b_im': nrm(ks[16], (L, S5_GROUPS, S5_STATE, S5_GROUP), (2.0 * S5_GROUP) ** -0.5),
        's5_c_re': nrm(ks[17], (L, S5_GROUPS, S5_GROUP, S5_STATE), S5_STATE ** -0.5),
        's5_c_im': nrm(ks[18], (L, S5_GROUPS, S5_GROUP, S5_STATE), S5_STATE ** -0.5),
        's5_d': nrm(ks[19], (L, S5_WIDTH), 1.0),
        'w_br_s5': nrm(ks[20], (L, S5_WIDTH, 2 * D_MODEL), S5_WIDTH ** -0.5),
        'gla_w_alpha': nrm(ks[21], (L, GLA_RANK, GLA_QK), GLA_RANK ** -0.5),
        'gla_b_alpha': nrm(ks[22], (L, GLA_QK), 0.02),
        'gla_norm': gain(ks[23], (L, GLA_DV)),
        'w_br_gla': nrm(ks[24], (L, GLA_VW, D_MODEL), GLA_VW ** -0.5),
        'w_o': nrm(ks[25], (L, D_MODEL, D_MODEL), DEEPNORM_BETA * D_MODEL ** -0.5),
        'ln1_g': gain(ks[26], (L, D_MODEL)),
        'ln1_b': nrm(ks[27], (L, D_MODEL), 0.02),
        'w_up': nrm(ks[28], (L, D_MODEL, 2 * D_FF), D_MODEL ** -0.5),
        'ffn_conv': nrm(ks[29], (L, FFN_CONV, 2 * D_FF), FFN_CONV ** -0.5),
        'w_down': nrm(ks[30], (L, D_FF, D_MODEL), DEEPNORM_BETA * D_FF ** -0.5),
        'ln2_g': gain(ks[31], (L, D_MODEL)),
        'ln2_b': nrm(ks[32], (L, D_MODEL), 0.02),
    }


def reference(x, w_in, dn_conv, dn_a_log, dn_dt_bias, dn_norm, w_br_dn,
              cf_dw, cf_dw_bias, cf_ln_g, cf_ln_b, w_br_cf,
              s5_a_re, s5_a_im, s5_log_dt, s5_b_re, s5_b_im, s5_c_re, s5_c_im, s5_d, w_br_s5,
              gla_w_alpha, gla_b_alpha, gla_norm, w_br_gla, w_o, ln1_g, ln1_b,
              w_up, ffn_conv, w_down, ln2_g, ln2_b):
    for l in range(DEPTH):
        mix = token_mixer(x, w_in[l], dn_conv[l], dn_a_log[l], dn_dt_bias[l], dn_norm[l], w_br_dn[l],
                          cf_dw[l], cf_dw_bias[l], cf_ln_g[l], cf_ln_b[l], w_br_cf[l],
                          s5_a_re[l], s5_a_im[l], s5_log_dt[l], s5_b_re[l], s5_b_im[l],
                          s5_c_re[l], s5_c_im[l], s5_d[l], w_br_s5[l],
                          gla_w_alpha[l], gla_b_alpha[l], gla_norm[l], w_br_gla[l], w_o[l])
        x = layer_norm(DEEPNORM_ALPHA * x + mix, ln1_g[l], ln1_b[l])
        x = layer_norm(DEEPNORM_ALPHA * x + conv_ffn(x, w_up[l], ffn_conv[l], w_down[l]), ln2_g[l], ln2_b[l])
    return x
```

```python
import functools
import math

import jax
import jax.numpy as jnp
from jax import lax
from jax.experimental import pallas as pl
from jax.experimental.pallas import tpu as pltpu

F32 = jnp.float32
BF16 = jnp.bfloat16

D_MODEL = 1024
DN_HEADS = 4
DN_DK = 128
DN_CONV = 4
DN_CHUNK = 64
DN_QKV = 1536
CF_WIDTH = 512
CF_KERNEL = 31
S5_WIDTH = 512
S5_GROUP = 16
S5_GROUPS = 32
S5_STATE = 64
S5_CHUNK = 32
GLA_HEADS = 4
GLA_DK = 64
GLA_DV = 128
GLA_QK = 256
GLA_VW = 512
GLA_RANK = 16
GLA_TAU = 16.0
GLA_CHUNK = 64
GLA_SUB = 16
N_BRANCH = 4
D_FF = 2816
FFN_CONV = 3
FFN_COLS = 256
LN_EPS = 1e-5
DEPTH = 4
DEEPNORM_ALPHA = (2.0 * DEPTH) ** 0.25

LANES = 128
SUBLANES = 8
SEQ_TILE = 256
PROJ_TILE = 256
VMEM_LIMIT = 56 * 1024 * 1024

C_QKV = 0
C_DNG = 1536
C_CF = 2048
C_S5 = 3072
C_GQK = 3584
C_GV = 4096
C_GG = 4608
C_GATE = 5120
W_BIG = 9216


def _dot(a, b):
    return jnp.dot(a.astype(BF16), b.astype(BF16), preferred_element_type=F32)


def _dot_nt(a, b):
    return lax.dot_general(a.astype(BF16), b.astype(BF16), (((1,), (1,)), ((), ())),
                           preferred_element_type=F32)


def _dot_tn(a, b):
    return lax.dot_general(a.astype(BF16), b.astype(BF16), (((0,), (0,)), ((), ())),
                           preferred_element_type=F32)


def _split3(x):
    x1 = x.astype(BF16)
    r = x - x1.astype(F32)
    x2 = r.astype(BF16)
    x3 = (r - x2.astype(F32)).astype(BF16)
    return x1, x2, x3


def _split2(x):
    x1 = x.astype(BF16)
    x2 = (x - x1.astype(F32)).astype(BF16)
    return x1, x2


def _dot01_left(m01, x):
    x1, x2, x3 = _split3(x)
    d = lambda v: jnp.dot(m01, v, preferred_element_type=F32)
    return d(x3) + d(x2) + d(x1)


def _dot01_right(x, m01):
    x1, x2, x3 = _split3(x)
    d = lambda v: jnp.dot(v, m01, preferred_element_type=F32)
    return d(x3) + d(x2) + d(x1)


def _dot_hilo(a, b_hi, b_lo):
    a_hi, a_lo = _split2(a)
    d = lambda u, v: jnp.dot(u, v, preferred_element_type=F32)
    return d(a_lo, b_hi) + d(a_hi, b_lo) + d(a_hi, b_hi)


def _sigmoid(x):
    return 1.0 / (1.0 + jnp.exp(-x))


def _silu(x):
    return x * _sigmoid(x)


def _softplus(x):
    return jnp.maximum(x, 0.0) + jnp.log(1.0 + jnp.exp(-jnp.abs(x)))


def _gelu_tanh(x):
    c = math.sqrt(2.0 / math.pi)
    return 0.5 * x * (1.0 + jnp.tanh(c * (x + 0.044715 * (x * x * x))))


def _layer_norm(x, g, b):
    mu = jnp.mean(x, axis=-1, keepdims=True)
    xc = x - mu
    var = jnp.mean(xc * xc, axis=-1, keepdims=True)
    return xc * lax.rsqrt(var + LN_EPS) * g + b


def _tri(n, kind):
    r = lax.broadcasted_iota(jnp.int32, (n, n), 0)
    c = lax.broadcasted_iota(jnp.int32, (n, n), 1)
    if kind == "lower":
        return r >= c
    if kind == "strict":
        return r > c
    if kind == "upper":
        return r <= c
    raise ValueError(kind)


def _const_spec(shape, layer):
    nd = len(shape)
    return pl.BlockSpec((None,) + tuple(shape), lambda *_: (layer,) + (0,) * nd,
                        pipeline_mode=pl.Buffered(1))


def _params(sem):
    return pltpu.CompilerParams(dimension_semantics=sem, vmem_limit_bytes=VMEM_LIMIT)


def _proj_kernel(x_ref, w_ref, wsh_ref, wsl_ref, vec_ref, walh_ref, wall_ref, bal_ref,
                 qkv_ref, dng_ref, cf_ref, s5_ref, gqk_ref, gv_ref, gg_ref, gates_ref,
                 small_ref, loga_ref):
    x = x_ref[...]
    xb = x.astype(BF16)

    def mm(c0, n):
        return jnp.dot(xb, w_ref[:, c0:c0 + n], preferred_element_type=F32)

    for j in range(DN_QKV // 512):
        qkv_ref[:, j * 512:(j + 1) * 512] = mm(C_QKV + j * 512, 512).astype(BF16)
    dng_ref[...] = _silu(mm(C_DNG, 512)).astype(BF16)
    cf_a = mm(C_CF, 512)
    cf_g = mm(C_CF + 512, 512)
    cf_ref[...] = (cf_a * _sigmoid(cf_g)).astype(BF16)
    s5_ref[...] = mm(C_S5, 512).astype(BF16)
    gqk_ref[:, 0:GLA_QK] = (mm(C_GQK, GLA_QK) * (GLA_DK ** -0.5)).astype(BF16)
    gqk_ref[:, GLA_QK:2 * GLA_QK] = mm(C_GQK + GLA_QK, GLA_QK).astype(BF16)
    gv_ref[...] = mm(C_GV, 512).astype(BF16)
    gg_ref[...] = _silu(mm(C_GG, 512)).astype(BF16)
    for j in range(N_BRANCH * D_MODEL // 512):
        gates_ref[:, j * 512:(j + 1) * 512] = _sigmoid(mm(C_GATE + j * 512, 512)).astype(BF16)

    x_lo = (x - xb.astype(F32)).astype(BF16)
    d = lambda u, v: jnp.dot(u, v, preferred_element_type=F32)
    s = d(x_lo, wsh_ref[...]) + d(xb, wsl_ref[...]) + d(xb, wsh_ref[...])
    lane = lax.broadcasted_iota(jnp.int32, s.shape, 1)
    neg_a = vec_ref[0:1, :]
    dt_b = vec_ref[1:2, :]
    g = neg_a * _softplus(s + dt_b)
    beta = _sigmoid(s)
    small = jnp.where(lane < DN_HEADS, g, jnp.where(lane < 2 * DN_HEADS, beta, s))
    small_ref[...] = small
    z = _dot_hilo(s, walh_ref[...], wall_ref[...]) + bal_ref[...]
    log_sig = jnp.minimum(z, 0.0) - jnp.log(1.0 + jnp.exp(-jnp.abs(z)))
    loga_ref[...] = log_sig * (1.0 / GLA_TAU)


def _proj_call(layer, x2, pp):
    t = x2.shape[0]
    tm = PROJ_TILE
    row = lambda n: pl.BlockSpec((tm, n), lambda i: (i, 0))
    outs = [(DN_QKV, BF16), (512, BF16), (512, BF16), (512, BF16), (512, BF16), (512, BF16),
            (512, BF16), (N_BRANCH * D_MODEL, BF16), (LANES, F32), (GLA_QK, F32)]
    return pl.pallas_call(
        _proj_kernel,
        grid=(t // tm,),
        in_specs=[row(D_MODEL),
                  _const_spec((D_MODEL, W_BIG), layer),
                  _const_spec((D_MODEL, LANES), layer),
                  _const_spec((D_MODEL, LANES), layer),
                  _const_spec((SUBLANES, LANES), layer),
                  _const_spec((LANES, GLA_QK), layer),
                  _const_spec((LANES, GLA_QK), layer),
                  _const_spec((1, GLA_QK), layer)],
        out_specs=[row(n) for n, _ in outs],
        out_shape=[jax.ShapeDtypeStruct((t, n), dt) for n, dt in outs],
        compiler_params=_params(("arbitrary",)),
        name="proj",
    )(x2, pp["w_big"], pp["ws_hi"], pp["ws_lo"], pp["dn_vec"], pp["wal_hi"], pp["wal_lo"],
      pp["b_alpha"])


def _dn_kernel(qkv_ref, small_ref, smallt_ref, gate_ref, convw_ref, norm_ref, out_ref,
               ext_ref, qkvc_ref, st_ref):
    tl = qkv_ref.shape[0]
    c = DN_CHUNK

    @pl.when(pl.program_id(1) == 0)
    def _():
        ext_ref[0:SUBLANES, :] = jnp.zeros((SUBLANES, DN_QKV), F32)
        st_ref[...] = jnp.zeros(st_ref.shape, F32)

    ext_ref[SUBLANES:SUBLANES + tl, :] = qkv_ref[...].astype(F32)
    for j in range(DN_QKV // LANES):
        sl = slice(j * LANES, (j + 1) * LANES)
        acc = convw_ref[DN_CONV - 1:DN_CONV, sl] * ext_ref[SUBLANES:SUBLANES + tl, sl]
        for k in range(DN_CONV - 1):
            off = SUBLANES - (DN_CONV - 1) + k
            acc = acc + convw_ref[k:k + 1, sl] * ext_ref[off:off + tl, sl]
        y = _silu(acc)
        if j < 2 * DN_HEADS:
            y = y * lax.rsqrt(jnp.sum(y * y, axis=-1, keepdims=True) + 1e-6)
            if j < DN_HEADS:
                y = y * (DN_DK ** -0.5)
        qkvc_ref[:, sl] = y
    ext_ref[0:SUBLANES, :] = ext_ref[tl:tl + SUBLANES, :]

    causal = _tri(c, "lower")
    strict = _tri(c, "strict")
    tril01 = causal.astype(BF16)
    triu01 = _tri(c, "upper").astype(BF16)
    eye = (lax.broadcasted_iota(jnp.int32, (c, c), 0)
           == lax.broadcasted_iota(jnp.int32, (c, c), 1)).astype(F32)
    norm = norm_ref[...]

    for ci in range(tl // c):
        r0 = ci * c
        sm = small_ref[r0:r0 + c, :]
        gc_all = _dot01_left(tril01, sm)
        gcr_all = _dot01_right(smallt_ref[ci], triu01)
        exp_gc = jnp.exp(gc_all)
        g_last = gc_all[c - 1:c, :]
        exp_rem = jnp.exp(g_last - gc_all)
        exp_last = jnp.exp(g_last)
        for h in range(DN_HEADS):
            hs = slice(h * LANES, (h + 1) * LANES)
            q = qkvc_ref[r0:r0 + c, hs]
            k = qkvc_ref[r0:r0 + c, 4 * LANES + h * LANES:4 * LANES + (h + 1) * LANES]
            v = qkvc_ref[r0:r0 + c, 8 * LANES + h * LANES:8 * LANES + (h + 1) * LANES]
            beta = sm[:, DN_HEADS + h:DN_HEADS + h + 1]
            diff = gc_all[:, h:h + 1] - gcr_all[h:h + 1, :]
            decay = jnp.where(causal, jnp.exp(jnp.where(causal, diff, 0.0)), 0.0)
            kb = k * beta
            n = jnp.where(strict, _dot_nt(kb, k) * decay, 0.0)
            m = -n
            p = eye + m
            for _ in range(int(math.log2(c)) - 1):
                m = _dot(m, m)
                p = p + _dot(p, m)
            rhs = jnp.concatenate([v * beta, kb * exp_gc[:, h:h + 1]], axis=1)
            sol = _dot(p, rhs)
            u = sol[:, :LANES]
            w = sol[:, LANES:]
            a = _dot_nt(q, k) * decay
            qd = q * exp_gc[:, h:h + 1]
            kd = k * exp_rem[:, h:h + 1]
            s = st_ref[h]
            v_new = u - _dot(w, s)
            o = _dot(qd, s) + _dot(a, v_new)
            st_ref[h] = s * exp_last[:, h:h + 1] + _dot_tn(kd, v_new)
            o = o * lax.rsqrt(jnp.mean(o * o, axis=-1, keepdims=True) + LN_EPS) * norm
            out_ref[r0:r0 + c, hs] = (o * gate_ref[r0:r0 + c, hs].astype(F32)).astype(BF16)


def _dn_call(layer, bsz, seqlen, qkv, small, smallt, gate, pp):
    tl = SEQ_TILE
    nl = seqlen // tl
    row = lambda n: pl.BlockSpec((tl, n), lambda b, l: (b * nl + l, 0))
    return pl.pallas_call(
        _dn_kernel,
        grid=(bsz, nl),
        in_specs=[row(DN_QKV), row(LANES),
                  pl.BlockSpec((tl // DN_CHUNK, SUBLANES, DN_CHUNK), lambda b, l: (b * nl + l, 0, 0)),
                  row(512),
                  _const_spec((DN_CONV, DN_QKV), layer),
                  _const_spec((1, LANES), layer)],
        out_specs=row(512),
        out_shape=jax.ShapeDtypeStruct((bsz * seqlen, 512), BF16),
        scratch_shapes=[pltpu.VMEM((tl + SUBLANES, DN_QKV), F32),
                        pltpu.VMEM((tl, DN_QKV), F32),
                        pltpu.VMEM((DN_HEADS, DN_DK, LANES), F32)],
        compiler_params=_params(("arbitrary", "arbitrary")),
        name="deltanet",
    )(qkv, small, smallt, gate, pp["dn_conv"], pp["dn_norm"])


def _gla_kernel(qk_ref, v_ref, g_ref, loga_ref, norm_ref, out_ref, st_ref):
    tl = qk_ref.shape[0]
    c = GLA_CHUNK
    nsub = c // GLA_SUB

    @pl.when(pl.program_id(1) == 0)
    def _():
        st_ref[...] = jnp.zeros(st_ref.shape, F32)

    tril01 = _tri(c, "lower").astype(BF16)
    rows = lax.broadcasted_iota(jnp.int32, (c, LANES), 0)
    lane = lax.broadcasted_iota(jnp.int32, (GLA_SUB, LANES), 1)
    srow = lax.broadcasted_iota(jnp.int32, (GLA_SUB, c), 0)
    scol = lax.broadcasted_iota(jnp.int32, (GLA_SUB, c), 1)
    st_r = lax.broadcasted_iota(jnp.int32, (2 * GLA_DV, LANES), 0) >= GLA_DV
    st_c = lax.broadcasted_iota(jnp.int32, (2 * GLA_DV, LANES), 1) >= GLA_DK
    st_mask = st_r == st_c
    norm = norm_ref[...]

    for ci in range(tl // c):
        r0 = ci * c
        gcum = _dot01_left(tril01, loga_ref[r0:r0 + c, :])
        q = qk_ref[r0:r0 + c, 0:GLA_QK].astype(F32)
        k = qk_ref[r0:r0 + c, GLA_QK:2 * GLA_QK].astype(F32)
        g_end = gcum[c - 1:c, :]
        refs = [jnp.zeros((1, GLA_QK), F32)] + [gcum[GLA_SUB * i - 1:GLA_SUB * i, :] for i in range(1, nsub)]
        ref_rows = jnp.concatenate([jnp.broadcast_to(r, (GLA_SUB, GLA_QK)) for r in refs], axis=0)
        qn = q * jnp.exp(gcum - ref_rows)
        qdec = q * jnp.exp(gcum)
        kdec = k * jnp.exp(g_end - gcum)
        for p in range(GLA_HEADS // 2):
            ps = slice(p * LANES, (p + 1) * LANES)
            kp = k[:, ps]
            gp = gcum[:, ps]
            sc = [[], []]
            for i in range(nsub):
                e = jnp.where(rows < GLA_SUB * (i + 1), refs[i][:, ps] - gp, 0.0)
                kn = (kp * jnp.exp(e)).astype(BF16)
                qi = qn[GLA_SUB * i:GLA_SUB * (i + 1), ps]
                for hh in range(2):
                    lhs = jnp.where((lane >= GLA_DK) if hh else (lane < GLA_DK), qi, 0.0)
                    s = _dot_nt(lhs, kn)
                    sc[hh].append(jnp.where(scol <= srow + GLA_SUB * i, s, 0.0))
            st = st_ref[p]
            o_inter = _dot_nt(qdec[:, ps], st)
            vp = v_ref[r0:r0 + c, p * 2 * GLA_DV:(p + 1) * 2 * GLA_DV]
            upd = jnp.where(st_mask, _dot_tn(vp, kdec[:, ps]), 0.0)
            st_ref[p] = st * jnp.exp(g_end[:, ps]) + upd
            for hh in range(2):
                h = 2 * p + hh
                hs = slice(h * GLA_DV, (h + 1) * GLA_DV)
                scores = jnp.concatenate(sc[hh], axis=0)
                o = _dot(scores, v_ref[r0:r0 + c, hs]) + o_inter[:, hh * GLA_DV:(hh + 1) * GLA_DV]
                o = o * lax.rsqrt(jnp.mean(o * o, axis=-1, keepdims=True) + LN_EPS) * norm
                out_ref[r0:r0 + c, hs] = (o * g_ref[r0:r0 + c, hs].astype(F32)).astype(BF16)


def _gla_call(layer, bsz, seqlen, gqk, gv, gg, loga, pp):
    tl = SEQ_TILE
    nl = seqlen // tl
    row = lambda n: pl.BlockSpec((tl, n), lambda b, l: (b * nl + l, 0))
    return pl.pallas_call(
        _gla_kernel,
        grid=(bsz, nl),
        in_specs=[row(2 * GLA_QK), row(GLA_VW), row(GLA_VW), row(GLA_QK),
                  _const_spec((1, LANES), layer)],
        out_specs=row(GLA_VW),
        out_shape=jax.ShapeDtypeStruct((bsz * seqlen, GLA_VW), BF16),
        scratch_shapes=[pltpu.VMEM((GLA_HEADS // 2, 2 * GLA_DV, 2 * GLA_DK), F32)],
        compiler_params=_params(("arbitrary", "arbitrary")),
        name="gla",
    )(gqk, gv, gg, loga, pp["gla_norm"])


def _s5_kernel(u_ref, t_ref, pr_ref, pi_ref, qr_ref, qi_ref, ac_ref, y_ref,
               sr_ref, si_ref, xr_ref, xi_ref):
    bsz = SUBLANES
    u = u_ref[...]
    nsteps = u.shape[0] // bsz
    sr_ref[...] = jnp.dot(u, pr_ref[...], preferred_element_type=F32)
    si_ref[...] = jnp.dot(u, pi_ref[...], preferred_element_type=F32)
    ar = ac_ref[0:1, :]
    ai = ac_ref[1:2, :]

    def step(i, carry):
        xr, xi = carry
        r = pl.multiple_of(i * bsz, bsz)
        xr_ref[pl.ds(r, bsz), :] = xr
        xi_ref[pl.ds(r, bsz), :] = xi
        sr = sr_ref[pl.ds(r, bsz), :]
        si = si_ref[pl.ds(r, bsz), :]
        return ar * xr - ai * xi + sr, ar * xi + ai * xr + si

    zero = jnp.zeros((bsz, S5_STATE), F32)
    lax.fori_loop(0, nsteps, step, (zero, zero))
    y_ref[...] = (jnp.dot(u, t_ref[...], preferred_element_type=F32)
                  + _dot(xr_ref[...], qr_ref[...])
                  + _dot(xi_ref[...], qi_ref[...])).astype(BF16)


def _s5_call(layer, u, pp):
    g, r, w = u.shape
    blk = lambda a, b: pl.BlockSpec((None, None, a, b), lambda i: (layer, i, 0, 0))
    return pl.pallas_call(
        _s5_kernel,
        grid=(g,),
        in_specs=[pl.BlockSpec((None, r, w), lambda i: (i, 0, 0)),
                  blk(w, w), blk(w, S5_STATE), blk(w, S5_STATE), blk(S5_STATE, w), blk(S5_STATE, w),
                  blk(2, S5_STATE)],
        out_specs=pl.BlockSpec((None, r, w), lambda i: (i, 0, 0)),
        out_shape=jax.ShapeDtypeStruct((g, r, w), BF16),
        scratch_shapes=[pltpu.VMEM((r, S5_STATE), F32)] * 4,
        compiler_params=_params(("arbitrary",)),
        name="s5",
    )(u, pp["s5_t"], pp["s5_pr"], pp["s5_pi"], pp["s5_qr"], pp["s5_qi"], pp["s5_ac"])


CF_HALO = 32


def _merge_kernel(x_ref, a_ref, cf_ref, z_ref, d_ref, gates_ref,
                  wdn_ref, wcf_ref, ws5_ref, wgla_ref, wo_ref,
                  cfw_ref, cfb_ref, cfg_ref, cfbeta_ref, lng_ref, lnb_ref,
                  out_ref, ext_ref, conv_ref):
    tl = x_ref.shape[0]

    @pl.when(pl.program_id(1) == 0)
    def _():
        ext_ref[0:CF_HALO, :] = jnp.zeros((CF_HALO, CF_WIDTH), F32)

    ext_ref[CF_HALO:CF_HALO + tl, :] = cf_ref[...].astype(F32)
    base = CF_HALO - (CF_KERNEL - 1)
    for j in range(CF_WIDTH // LANES):
        sl = slice(j * LANES, (j + 1) * LANES)
        acc = cfw_ref[0:1, sl] * ext_ref[base:base + tl, sl]
        for k in range(1, CF_KERNEL):
            acc = acc + cfw_ref[k:k + 1, sl] * ext_ref[base + k:base + k + tl, sl]
        conv_ref[:, sl] = acc
    ext_ref[0:CF_HALO, :] = ext_ref[tl:tl + CF_HALO, :]

    cfo = _silu(_layer_norm(conv_ref[...] + cfb_ref[...], cfg_ref[...], cfbeta_ref[...]))
    y_b = jnp.dot(cfo.astype(BF16), wcf_ref[...], preferred_element_type=F32)
    y_a = jnp.dot(a_ref[...], wdn_ref[...], preferred_element_type=F32)
    y_d = jnp.dot(d_ref[...], wgla_ref[...], preferred_element_type=F32)
    zg = _gelu_tanh(z_ref[...].astype(F32)).astype(BF16)
    z_val = jnp.dot(zg, ws5_ref[:, 0:D_MODEL], preferred_element_type=F32)
    z_gate = jnp.dot(zg, ws5_ref[:, D_MODEL:2 * D_MODEL], preferred_element_type=F32)
    y_c = z_val * _sigmoid(z_gate)
    gt = lambda i: gates_ref[:, i * D_MODEL:(i + 1) * D_MODEL].astype(F32)
    merged = gt(0) * y_a + gt(1) * y_b + gt(2) * y_c + gt(3) * y_d
    mix = jnp.dot(merged.astype(BF16), wo_ref[...], preferred_element_type=F32)
    out_ref[...] = _layer_norm(DEEPNORM_ALPHA * x_ref[...] + mix, lng_ref[...], lnb_ref[...])


def _merge_call(layer, bsz, seqlen, x2, a_in, cf, z, d_in, gates, pp):
    tl = SEQ_TILE
    nl = seqlen // tl
    row = lambda n: pl.BlockSpec((tl, n), lambda b, l: (b * nl + l, 0))
    return pl.pallas_call(
        _merge_kernel,
        grid=(bsz, nl),
        in_specs=[row(D_MODEL), row(512), row(CF_WIDTH), row(S5_WIDTH), row(GLA_VW),
                  row(N_BRANCH * D_MODEL),
                  _const_spec((512, D_MODEL), layer), _const_spec((CF_WIDTH, D_MODEL), layer),
                  _const_spec((S5_WIDTH, 2 * D_MODEL), layer), _const_spec((GLA_VW, D_MODEL), layer),
                  _const_spec((D_MODEL, D_MODEL), layer),
                  _const_spec((CF_HALO, CF_WIDTH), layer), _const_spec((1, CF_WIDTH), layer),
                  _const_spec((1, CF_WIDTH), layer), _const_spec((1, CF_WIDTH), layer),
                  _const_spec((1, D_MODEL), layer), _const_spec((1, D_MODEL), layer)],
        out_specs=row(D_MODEL),
        out_shape=jax.ShapeDtypeStruct((bsz * seqlen, D_MODEL), F32),
        scratch_shapes=[pltpu.VMEM((tl + CF_HALO, CF_WIDTH), F32),
                        pltpu.VMEM((tl, CF_WIDTH), F32)],
        compiler_params=_params(("arbitrary", "arbitrary")),
        name="merge",
    )(x2, a_in, cf, z, d_in, gates, pp["w_br_dn"], pp["w_br_cf"], pp["w_br_s5"], pp["w_br_gla"],
      pp["w_o"], pp["cf_dw"], pp["cf_dw_bias"], pp["cf_ln_g"], pp["cf_ln_b"], pp["ln1_g"], pp["ln1_b"])


def _ffn_kernel(x_ref, wup_ref, convw_ref, wdown_ref, lng_ref, lnb_ref, out_ref,
                halo_ref, ext_ref):
    tl = x_ref.shape[0]
    w = FFN_COLS

    @pl.when(pl.program_id(1) == 0)
    def _():
        halo_ref[...] = jnp.zeros(halo_ref.shape, F32)

    x = x_ref[...]
    xb = x.astype(BF16)
    acc = jnp.zeros((tl, D_MODEL), F32)
    for ci in range(D_FF // w):
        halves = []
        for half in range(2):
            c0 = half * D_FF + ci * w
            cs = slice(c0, c0 + w)
            u = jnp.dot(xb, wup_ref[:, cs], preferred_element_type=F32)
            ext_ref[0:SUBLANES, :] = halo_ref[:, cs]
            ext_ref[SUBLANES:SUBLANES + tl, :] = u
            halo_ref[:, cs] = u[tl - SUBLANES:tl, :]
            y = convw_ref[FFN_CONV - 1:FFN_CONV, cs] * u
            for k in range(FFN_CONV - 1):
                off = SUBLANES - (FFN_CONV - 1) + k
                y = y + convw_ref[k:k + 1, cs] * ext_ref[off:off + tl, :]
            halves.append(y)
        hidden = (_silu(halves[0]) * halves[1]).astype(BF16)
        acc = acc + jnp.dot(hidden, wdown_ref[ci * w:(ci + 1) * w, :], preferred_element_type=F32)
    out_ref[...] = _layer_norm(DEEPNORM_ALPHA * x + acc, lng_ref[...], lnb_ref[...])


def _ffn_call(layer, bsz, seqlen, x2, pp):
    tl = SEQ_TILE
    nl = seqlen // tl
    row = lambda n: pl.BlockSpec((tl, n), lambda b, l: (b * nl + l, 0))
    return pl.pallas_call(
        _ffn_kernel,
        grid=(bsz, nl),
        in_specs=[row(D_MODEL),
                  _const_spec((D_MODEL, 2 * D_FF), layer), _const_spec((SUBLANES, 2 * D_FF), layer),
                  _const_spec((D_FF, D_MODEL), layer),
                  _const_spec((1, D_MODEL), layer), _const_spec((1, D_MODEL), layer)],
        out_specs=row(D_MODEL),
        out_shape=jax.ShapeDtypeStruct((bsz * seqlen, D_MODEL), F32),
        scratch_shapes=[pltpu.VMEM((SUBLANES, 2 * D_FF), F32),
                        pltpu.VMEM((tl + SUBLANES, FFN_COLS), F32)],
        compiler_params=_params(("arbitrary", "arbitrary")),
        name="convffn",
    )(x2, pp["w_up"], pp["ffn_conv"], pp["w_down"], pp["ln2_g"], pp["ln2_b"])


def _pad_rows(a, rows):
    return jnp.pad(a, ((0, 0), (0, rows - a.shape[1]), (0, 0)))


def _s5_tables(a_re, a_im, log_dt, b_re, b_im, c_re, c_im, d):
    c = S5_CHUNK
    dt = jnp.exp(log_dt)[..., None]
    mag = jnp.exp(dt * a_re)
    abar_re, abar_im = mag * jnp.cos(dt * a_im), mag * jnp.sin(dt * a_im)
    den = a_re * a_re + a_im * a_im
    nr, ni = abar_re - 1.0, abar_im
    fr, fi = (nr * a_re + ni * a_im) / den, (ni * a_re - nr * a_im) / den
    bb_re = fr[..., None] * b_re - fi[..., None] * b_im
    bb_im = fr[..., None] * b_im + fi[..., None] * b_re
    j = jnp.arange(c + 1, dtype=F32)[:, None, None, None]
    pmag = jnp.exp(j * (dt * a_re)[None])
    pw_re = pmag * jnp.cos(j * (dt * a_im)[None])
    pw_im = pmag * jnp.sin(j * (dt * a_im)[None])
    cb_re = jnp.einsum('lgon,lgni->lgnoi', c_re, bb_re) - jnp.einsum('lgon,lgni->lgnoi', c_im, bb_im)
    cb_im = jnp.einsum('lgon,lgni->lgnoi', c_re, bb_im) + jnp.einsum('lgon,lgni->lgnoi', c_im, bb_re)
    kern = (jnp.einsum('jlgn,lgnoi->jlgoi', pw_re[:c], cb_re)
            - jnp.einsum('jlgn,lgnoi->jlgoi', pw_im[:c], cb_im))
    eye = jnp.eye(S5_GROUP, dtype=F32)
    kern = kern.at[0].add(d.reshape(d.shape[0], S5_GROUPS, S5_GROUP)[..., None] * eye)
    s_idx = jnp.arange(c)[:, None]
    t_idx = jnp.arange(c)[None, :]
    lag = t_idx - s_idx
    toe = jnp.where((lag >= 0)[..., None, None, None, None], kern[jnp.clip(lag, 0, c - 1)], 0.0)
    toe = toe.transpose(2, 3, 0, 5, 1, 4).reshape(d.shape[0], S5_GROUPS, c * S5_GROUP, c * S5_GROUP)
    rp_re, rp_im = pw_re[:c][::-1], pw_im[:c][::-1]
    p_re = rp_re[..., None] * bb_re[None] - rp_im[..., None] * bb_im[None]
    p_im = rp_re[..., None] * bb_im[None] + rp_im[..., None] * bb_re[None]
    tos = lambda a: a.transpose(1, 2, 0, 4, 3).reshape(d.shape[0], S5_GROUPS, c * S5_GROUP, S5_STATE)
    q_re = (jnp.einsum('lgon,tlgn->lgnto', c_re, pw_re[1:]) - jnp.einsum('lgon,tlgn->lgnto', c_im, pw_im[1:]))
    q_im = -(jnp.einsum('lgon,tlgn->lgnto', c_re, pw_im[1:]) + jnp.einsum('lgon,tlgn->lgnto', c_im, pw_re[1:]))
    toq = lambda a: a.reshape(d.shape[0], S5_GROUPS, S5_STATE, c * S5_GROUP)
    ac = jnp.stack([pw_re[c], pw_im[c]], axis=2)
    return dict(s5_t=toe.astype(BF16), s5_pr=tos(p_re).astype(BF16), s5_pi=tos(p_im).astype(BF16),
                s5_qr=toq(q_re).astype(BF16), s5_qi=toq(q_im).astype(BF16), s5_ac=ac)


def _pack_params(w_in, dn_conv, dn_a_log, dn_dt_bias, dn_norm, w_br_dn, cf_dw, cf_dw_bias, cf_ln_g,
                 cf_ln_b, w_br_cf, s5_a_re, s5_a_im, s5_log_dt, s5_b_re, s5_b_im, s5_c_re, s5_c_im,
                 s5_d, w_br_s5, gla_w_alpha, gla_b_alpha, gla_norm, w_br_gla, w_o, ln1_g, ln1_b,
                 w_up, ffn_conv, w_down, ln2_g, ln2_b):
    nl = w_in.shape[0]
    sizes = (DN_QKV, DN_HEADS, DN_HEADS, 512, 2 * CF_WIDTH, S5_WIDTH, GLA_QK, GLA_QK, GLA_VW, GLA_VW,
             GLA_RANK, N_BRANCH * D_MODEL)
    offs = [0]
    for s in sizes:
        offs.append(offs[-1] + s)
    col = lambda i: w_in[:, :, offs[i]:offs[i + 1]]
    w_big = jnp.concatenate([col(0), col(3), col(4), col(5), col(6), col(7), col(8), col(9), col(11)],
                            axis=-1).astype(BF16)
    w_small = jnp.concatenate([col(1), col(2), col(10)], axis=-1)
    w_small = jnp.pad(w_small, ((0, 0), (0, 0), (0, LANES - w_small.shape[-1])))
    ws_hi = w_small.astype(BF16)
    ws_lo = (w_small - ws_hi.astype(F32)).astype(BF16)
    pad_l = lambda a: jnp.pad(a, ((0, 0), (0, LANES - a.shape[-1])))
    dn_vec = jnp.stack([pad_l(-jnp.exp(dn_a_log)), pad_l(dn_dt_bias)], axis=1)
    dn_vec = _pad_rows(dn_vec, SUBLANES)
    wal = jnp.pad(gla_w_alpha, ((0, 0), (2 * DN_HEADS, LANES - 2 * DN_HEADS - GLA_RANK), (0, 0)))
    wal_hi = wal.astype(BF16)
    wal_lo = (wal - wal_hi.astype(F32)).astype(BF16)
    vec = lambda a: a[:, None, :]
    pp = dict(
        w_big=w_big, ws_hi=ws_hi, ws_lo=ws_lo, dn_vec=dn_vec, wal_hi=wal_hi, wal_lo=wal_lo,
        b_alpha=vec(gla_b_alpha),
        dn_conv=dn_conv, dn_norm=vec(dn_norm), gla_norm=vec(gla_norm),
        w_br_dn=w_br_dn.astype(BF16), w_br_cf=w_br_cf.astype(BF16), w_br_s5=w_br_s5.astype(BF16),
        w_br_gla=w_br_gla.astype(BF16), w_o=w_o.astype(BF16),
        cf_dw=_pad_rows(cf_dw, CF_HALO), cf_dw_bias=vec(cf_dw_bias), cf_ln_g=vec(cf_ln_g),
        cf_ln_b=vec(cf_ln_b), ln1_g=vec(ln1_g), ln1_b=vec(ln1_b),
        w_up=w_up.astype(BF16), ffn_conv=_pad_rows(ffn_conv, SUBLANES), w_down=w_down.astype(BF16),
        ln2_g=vec(ln2_g), ln2_b=vec(ln2_b),
    )
    pp.update(_s5_tables(s5_a_re, s5_a_im, s5_log_dt, s5_b_re, s5_b_im, s5_c_re, s5_c_im, s5_d))
    del nl
    return pp


def kernel(x, w_in, dn_conv, dn_a_log, dn_dt_bias, dn_norm, w_br_dn, cf_dw, cf_dw_bias, cf_ln_g, cf_ln_b, w_br_cf, s5_a_re, s5_a_im, s5_log_dt, s5_b_re, s5_b_im, s5_c_re, s5_c_im, s5_d, w_br_s5, gla_w_alpha, gla_b_alpha, gla_norm, w_br_gla, w_o, ln1_g, ln1_b, w_up, ffn_conv, w_down, ln2_g, ln2_b):
    bsz, seqlen, d_model = x.shape
    assert d_model == D_MODEL and bsz == SUBLANES
    assert seqlen % SEQ_TILE == 0 and (bsz * seqlen) % PROJ_TILE == 0
    depth = w_in.shape[0]
    pp = _pack_params(w_in, dn_conv, dn_a_log, dn_dt_bias, dn_norm, w_br_dn, cf_dw, cf_dw_bias,
                      cf_ln_g, cf_ln_b, w_br_cf, s5_a_re, s5_a_im, s5_log_dt, s5_b_re, s5_b_im,
                      s5_c_re, s5_c_im, s5_d, w_br_s5, gla_w_alpha, gla_b_alpha, gla_norm, w_br_gla,
                      w_o, ln1_g, ln1_b, w_up, ffn_conv, w_down, ln2_g, ln2_b)
    t = bsz * seqlen
    nch = seqlen // S5_CHUNK
    x2 = x.reshape(t, D_MODEL)
    for layer in range(depth):
        (qkv, dng, cf, s5_in, gqk, gv, gg, gates, small, loga) = _proj_call(layer, x2, pp)
        smallt = small[:, :SUBLANES].reshape(t // DN_CHUNK, DN_CHUNK, SUBLANES).transpose(0, 2, 1)
        a_in = _dn_call(layer, bsz, seqlen, qkv, small, smallt, dng, pp)
        d_in = _gla_call(layer, bsz, seqlen, gqk, gv, gg, loga, pp)
        u = s5_in.reshape(bsz, nch, S5_CHUNK, S5_GROUPS, S5_GROUP).transpose(3, 1, 0, 2, 4)
        u = u.reshape(S5_GROUPS, nch * bsz, S5_CHUNK * S5_GROUP)
        y = _s5_call(layer, u, pp)
        z = y.reshape(S5_GROUPS, nch, bsz, S5_CHUNK, S5_GROUP).transpose(2, 1, 3, 0, 4)
        z = z.reshape(t, S5_WIDTH)
        x2 = _merge_call(layer, bsz, seqlen, x2, a_in, cf, z, d_in, gates, pp)
        x2 = _ffn_call(layer, bsz, seqlen, x2, pp)
    return x2.reshape(bsz, seqlen, D_MODEL)
```

```python
import functools
import math

import jax
import jax.numpy as jnp
from jax import lax
from jax.experimental import pallas as pl
from jax.experimental.pallas import tpu as pltpu

F32 = jnp.float32
BF16 = jnp.bfloat16

D_MODEL = 1024
DN_HEADS = 4
DN_DK = 128
DN_CONV = 4
DN_CHUNK = 64
DN_QKV = 1536
CF_WIDTH = 512
CF_KERNEL = 31
S5_WIDTH = 512
S5_GROUP = 16
S5_GROUPS = 32
S5_STATE = 64
S5_CHUNK = 32
GLA_HEADS = 4
GLA_DK = 64
GLA_DV = 128
GLA_QK = 256
GLA_VW = 512
GLA_RANK = 16
GLA_TAU = 16.0
GLA_CHUNK = 64
GLA_SUB = 16
N_BRANCH = 4
D_FF = 2816
FFN_CONV = 3
FFN_COLS = 256
LN_EPS = 1e-5
DEPTH = 4
DEEPNORM_ALPHA = (2.0 * DEPTH) ** 0.25

LANES = 128
SUBLANES = 8
SEQ_TILE = 256
PROJ_TILE = 256
VMEM_LIMIT = 56 * 1024 * 1024

C_QKV = 0
C_DNG = 1536
C_CF = 2048
C_S5 = 3072
C_GQK = 3584
C_GV = 4096
C_GG = 4608
C_GATE = 5120
W_BIG = 9216


def _dot(a, b):
    return jnp.dot(a.astype(BF16), b.astype(BF16), preferred_element_type=F32)


def _dot_nt(a, b):
    return lax.dot_general(a.astype(BF16), b.astype(BF16), (((1,), (1,)), ((), ())),
                           preferred_element_type=F32)


def _dot_tn(a, b):
    return lax.dot_general(a.astype(BF16), b.astype(BF16), (((0,), (0,)), ((), ())),
                           preferred_element_type=F32)


def _split3(x):
    x1 = x.astype(BF16)
    r = x - x1.astype(F32)
    x2 = r.astype(BF16)
    x3 = (r - x2.astype(F32)).astype(BF16)
    return x1, x2, x3


def _split2(x):
    x1 = x.astype(BF16)
    x2 = (x - x1.astype(F32)).astype(BF16)
    return x1, x2


def _dot01_left(m01, x):
    x1, x2, x3 = _split3(x)
    d = lambda v: jnp.dot(m01, v, preferred_element_type=F32)
    return d(x3) + d(x2) + d(x1)


def _dot01_right(x, m01):
    x1, x2, x3 = _split3(x)
    d = lambda v: jnp.dot(v, m01, preferred_element_type=F32)
    return d(x3) + d(x2) + d(x1)


def _dot_hilo(a, b_hi, b_lo):
    a_hi, a_lo = _split2(a)
    d = lambda u, v: jnp.dot(u, v, preferred_element_type=F32)
    return d(a_lo, b_hi) + d(a_hi, b_lo) + d(a_hi, b_hi)


def _sigmoid(x):
    return 1.0 / (1.0 + jnp.exp(-x))


def _silu(x):
    return x * _sigmoid(x)


def _softplus(x):
    return jnp.maximum(x, 0.0) + jnp.log(1.0 + jnp.exp(-jnp.abs(x)))


def _gelu_tanh(x):
    c = math.sqrt(2.0 / math.pi)
    return 0.5 * x * (1.0 + jnp.tanh(c * (x + 0.044715 * (x * x * x))))


def _layer_norm(x, g, b):
    mu = jnp.mean(x, axis=-1, keepdims=True)
    xc = x - mu
    var = jnp.mean(xc * xc, axis=-1, keepdims=True)
    return xc * lax.rsqrt(var + LN_EPS) * g + b


def _tri(n, kind):
    r = lax.broadcasted_iota(jnp.int32, (n, n), 0)
    c = lax.broadcasted_iota(jnp.int32, (n, n), 1)
    if kind == "lower":
        return r >= c
    if kind == "strict":
        return r > c
    if kind == "upper":
        return r <= c
    raise ValueError(kind)


def _const_spec(shape, layer):
    nd = len(shape)
    return pl.BlockSpec((None,) + tuple(shape), lambda *_: (layer,) + (0,) * nd,
                        pipeline_mode=pl.Buffered(1))


def _params(sem):
    return pltpu.CompilerParams(dimension_semantics=sem, vmem_limit_bytes=VMEM_LIMIT)


def _proj_kernel(x_ref, w_ref, wsh_ref, wsl_ref, vec_ref, walh_ref, wall_ref, bal_ref,
                 qkv_ref, dng_ref, cf_ref, s5_ref, gqk_ref, gv_ref, gg_ref, gates_ref,
                 small_ref, loga_ref):
    x = x_ref[...]
    xb = x.astype(BF16)

    def mm(c0, n):
        return jnp.dot(xb, w_ref[:, c0:c0 + n], preferred_element_type=F32)

    for j in range(DN_QKV // 512):
        qkv_ref[:, j * 512:(j + 1) * 512] = mm(C_QKV + j * 512, 512).astype(BF16)
    dng_ref[...] = _silu(mm(C_DNG, 512)).astype(BF16)
    cf_a = mm(C_CF, 512)
    cf_g = mm(C_CF + 512, 512)
    cf_ref[...] = (cf_a * _sigmoid(cf_g)).astype(BF16)
    s5_ref[...] = mm(C_S5, 512).astype(BF16)
    gqk_ref[:, 0:GLA_QK] = (mm(C_GQK, GLA_QK) * (GLA_DK ** -0.5)).astype(BF16)
    gqk_ref[:, GLA_QK:2 * GLA_QK] = mm(C_GQK + GLA_QK, GLA_QK).astype(BF16)
    gv_ref[...] = mm(C_GV, 512).astype(BF16)
    gg_ref[...] = _silu(mm(C_GG, 512)).astype(BF16)
    for j in range(N_BRANCH * D_MODEL // 512):
        gates_ref[:, j * 512:(j + 1) * 512] = _sigmoid(mm(C_GATE + j * 512, 512)).astype(BF16)

    x_lo = (x - xb.astype(F32)).astype(BF16)
    d = lambda u, v: jnp.dot(u, v, preferred_element_type=F32)
    s = d(x_lo, wsh_ref[...]) + d(xb, wsl_ref[...]) + d(xb, wsh_ref[...])
    lane = lax.broadcasted_iota(jnp.int32, s.shape, 1)
    neg_a = vec_ref[0:1, :]
    dt_b = vec_ref[1:2, :]
    g = neg_a * _softplus(s + dt_b)
    beta = _sigmoid(s)
    small = jnp.where(lane < DN_HEADS, g, jnp.where(lane < 2 * DN_HEADS, beta, s))
    small_ref[...] = small
    z = _dot_hilo(s, walh_ref[...], wall_ref[...]) + bal_ref[...]
    log_sig = jnp.minimum(z, 0.0) - jnp.log(1.0 + jnp.exp(-jnp.abs(z)))
    loga_ref[...] = log_sig * (1.0 / GLA_TAU)


def _proj_call(layer, x2, pp):
    t = x2.shape[0]
    tm = PROJ_TILE
    row = lambda n: pl.BlockSpec((tm, n), lambda i: (i, 0))
    outs = [(DN_QKV, BF16), (512, BF16), (512, BF16), (512, BF16), (512, BF16), (512, BF16),
            (512, BF16), (N_BRANCH * D_MODEL, BF16), (LANES, F32), (GLA_QK, F32)]
    return pl.pallas_call(
        _proj_kernel,
        grid=(t // tm,),
        in_specs=[row(D_MODEL),
                  _const_spec((D_MODEL, W_BIG), layer),
                  _const_spec((D_MODEL, LANES), layer),
                  _const_spec((D_MODEL, LANES), layer),
                  _const_spec((SUBLANES, LANES), layer),
                  _const_spec((LANES, GLA_QK), layer),
                  _const_spec((LANES, GLA_QK), layer),
                  _const_spec((1, GLA_QK), layer)],
        out_specs=[row(n) for n, _ in outs],
        out_shape=[jax.ShapeDtypeStruct((t, n), dt) for n, dt in outs],
        compiler_params=_params(("arbitrary",)),
        name="proj",
    )(x2, pp["w_big"], pp["ws_hi"], pp["ws_lo"], pp["dn_vec"], pp["wal_hi"], pp["wal_lo"],
      pp["b_alpha"])


def _dn_kernel(qkv_ref, small_ref, smallt_ref, gate_ref, convw_ref, norm_ref, out_ref,
               ext_ref, qkvc_ref, st_ref):
    tl = qkv_ref.shape[0]
    c = DN_CHUNK

    @pl.when(pl.program_id(1) == 0)
    def _():
        ext_ref[0:SUBLANES, :] = jnp.zeros((SUBLANES, DN_QKV), F32)
        st_ref[...] = jnp.zeros(st_ref.shape, F32)

    ext_ref[SUBLANES:SUBLANES + tl, :] = qkv_ref[...].astype(F32)
    for j in range(DN_QKV // LANES):
        sl = slice(j * LANES, (j + 1) * LANES)
        acc = convw_ref[DN_CONV - 1:DN_CONV, sl] * ext_ref[SUBLANES:SUBLANES + tl, sl]
        for k in range(DN_CONV - 1):
            off = SUBLANES - (DN_CONV - 1) + k
            acc = acc + convw_ref[k:k + 1, sl] * ext_ref[off:off + tl, sl]
        y = _silu(acc)
        if j < 2 * DN_HEADS:
            y = y * lax.rsqrt(jnp.sum(y * y, axis=-1, keepdims=True) + 1e-6)
            if j < DN_HEADS:
                y = y * (DN_DK ** -0.5)
        qkvc_ref[:, sl] = y
    ext_ref[0:SUBLANES, :] = ext_ref[tl:tl + SUBLANES, :]

    causal = _tri(c, "lower")
    strict = _tri(c, "strict")
    tril01 = causal.astype(BF16)
    triu01 = _tri(c, "upper").astype(BF16)
    eye = (lax.broadcasted_iota(jnp.int32, (c, c), 0)
           == lax.broadcasted_iota(jnp.int32, (c, c), 1)).astype(F32)
    norm = norm_ref[...]

    items = [(ci, h) for ci in range(tl // c) for h in range(DN_HEADS)]
    ms, rhs, amat, qds, kds, lasts = [], [], [], [], [], []
    for ci in range(tl // c):
        r0 = ci * c
        sm = small_ref[r0:r0 + c, :]
        gc_all = _dot01_left(tril01, sm)
        gcr_all = _dot01_right(smallt_ref[ci], triu01)
        exp_gc = jnp.exp(gc_all)
        g_last = gc_all[c - 1:c, :]
        exp_rem = jnp.exp(g_last - gc_all)
        exp_last = jnp.exp(g_last)
        for h in range(DN_HEADS):
            q = qkvc_ref[r0:r0 + c, h * LANES:(h + 1) * LANES]
            k = qkvc_ref[r0:r0 + c, (4 + h) * LANES:(5 + h) * LANES]
            v = qkvc_ref[r0:r0 + c, (8 + h) * LANES:(9 + h) * LANES]
            beta = sm[:, DN_HEADS + h:DN_HEADS + h + 1]
            diff = gc_all[:, h:h + 1] - gcr_all[h:h + 1, :]
            decay = jnp.where(causal, jnp.exp(jnp.where(causal, diff, 0.0)), 0.0)
            kb = k * beta
            kbf = k.astype(BF16)
            ms.append(-jnp.where(strict, _dot_nt(kb, kbf) * decay, 0.0))
            amat.append((_dot_nt(q, kbf) * decay).astype(BF16))
            rhs.append(jnp.concatenate([v * beta, kb * exp_gc[:, h:h + 1]], axis=1).astype(BF16))
            qds.append(q * exp_gc[:, h:h + 1])
            kds.append((k * exp_rem[:, h:h + 1]).astype(BF16))
            lasts.append(exp_last[:, h:h + 1])
    ps = [eye + m for m in ms]
    for _ in range(int(math.log2(c)) - 1):
        ms = [_dot(m, m) for m in ms]
        ps = [p + _dot(p, m) for p, m in zip(ps, ms)]
    sols = [_dot(p, r) for p, r in zip(ps, rhs)]
    us = [s[:, :LANES].astype(BF16) for s in sols]
    ws = [s[:, LANES:].astype(BF16) for s in sols]
    qeff = [qd - _dot(a, w) for qd, a, w in zip(qds, amat, ws)]
    oc = [_dot(a, u) for a, u in zip(amat, us)]
    gmat = [_dot_tn(kd, w) for kd, w in zip(kds, ws)]
    bmat = [_dot_tn(kd, u) for kd, u in zip(kds, us)]
    for idx, (ci, h) in enumerate(items):
        r0 = ci * c
        hs = slice(h * LANES, (h + 1) * LANES)
        s = st_ref[h]
        sb = s.astype(BF16)
        o = _dot(qeff[idx], sb) + oc[idx]
        st_ref[h] = s * lasts[idx] - _dot(gmat[idx], sb) + bmat[idx]
        o = o * lax.rsqrt(jnp.mean(o * o, axis=-1, keepdims=True) + LN_EPS) * norm
        out_ref[r0:r0 + c, hs] = (o * gate_ref[r0:r0 + c, hs].astype(F32)).astype(BF16)


def _dn_call(layer, bsz, seqlen, qkv, small, smallt, gate, pp):
    tl = SEQ_TILE
    nl = seqlen // tl
    row = lambda n: pl.BlockSpec((tl, n), lambda b, l: (b * nl + l, 0))
    return pl.pallas_call(
        _dn_kernel,
        grid=(bsz, nl),
        in_specs=[row(DN_QKV), row(LANES),
                  pl.BlockSpec((tl // DN_CHUNK, SUBLANES, DN_CHUNK), lambda b, l: (b * nl + l, 0, 0)),
                  row(512),
                  _const_spec((DN_CONV, DN_QKV), layer),
                  _const_spec((1, LANES), layer)],
        out_specs=row(512),
        out_shape=jax.ShapeDtypeStruct((bsz * seqlen, 512), BF16),
        scratch_shapes=[pltpu.VMEM((tl + SUBLANES, DN_QKV), F32),
                        pltpu.VMEM((tl, DN_QKV), F32),
                        pltpu.VMEM((DN_HEADS, DN_DK, LANES), F32)],
        compiler_params=_params(("arbitrary", "arbitrary")),
        name="deltanet",
    )(qkv, small, smallt, gate, pp["dn_conv"], pp["dn_norm"])


def _gla_kernel(qk_ref, v_ref, g_ref, loga_ref, norm_ref, out_ref, st_ref):
    tl = qk_ref.shape[0]
    c = GLA_CHUNK
    nsub = c // GLA_SUB

    @pl.when(pl.program_id(1) == 0)
    def _():
        st_ref[...] = jnp.zeros(st_ref.shape, F32)

    tril01 = _tri(c, "lower").astype(BF16)
    rows = lax.broadcasted_iota(jnp.int32, (c, LANES), 0)
    lane = lax.broadcasted_iota(jnp.int32, (GLA_SUB, LANES), 1)
    srow = lax.broadcasted_iota(jnp.int32, (GLA_SUB, c), 0)
    scol = lax.broadcasted_iota(jnp.int32, (GLA_SUB, c), 1)
    st_r = lax.broadcasted_iota(jnp.int32, (2 * GLA_DV, LANES), 0) >= GLA_DV
    st_c = lax.broadcasted_iota(jnp.int32, (2 * GLA_DV, LANES), 1) >= GLA_DK
    st_mask = st_r == st_c
    norm = norm_ref[...]

    for ci in range(tl // c):
        r0 = ci * c
        gcum = _dot01_left(tril01, loga_ref[r0:r0 + c, :])
        q = qk_ref[r0:r0 + c, 0:GLA_QK].astype(F32)
        k = qk_ref[r0:r0 + c, GLA_QK:2 * GLA_QK].astype(F32)
        g_end = gcum[c - 1:c, :]
        refs = [jnp.zeros((1, GLA_QK), F32)] + [gcum[GLA_SUB * i - 1:GLA_SUB * i, :] for i in range(1, nsub)]
        ref_rows = jnp.concatenate([jnp.broadcast_to(r, (GLA_SUB, GLA_QK)) for r in refs], axis=0)
        qn = q * jnp.exp(gcum - ref_rows)
        qdec = q * jnp.exp(gcum)
        kdec = k * jnp.exp(g_end - gcum)
        for p in range(GLA_HEADS // 2):
            ps = slice(p * LANES, (p + 1) * LANES)
            kp = k[:, ps]
            gp = gcum[:, ps]
            sc = [[], []]
            for i in range(nsub):
                e = jnp.where(rows < GLA_SUB * (i + 1), refs[i][:, ps] - gp, 0.0)
                kn = (kp * jnp.exp(e)).astype(BF16)
                qi = qn[GLA_SUB * i:GLA_SUB * (i + 1), ps]
                for hh in range(2):
                    lhs = jnp.where((lane >= GLA_DK) if hh else (lane < GLA_DK), qi, 0.0)
                    s = _dot_nt(lhs, kn)
                    sc[hh].append(jnp.where(scol <= srow + GLA_SUB * i, s, 0.0))
            st = st_ref[p]
            o_inter = _dot_nt(qdec[:, ps], st)
            vp = v_ref[r0:r0 + c, p * 2 * GLA_DV:(p + 1) * 2 * GLA_DV]
            upd = jnp.where(st_mask, _dot_tn(vp, kdec[:, ps]), 0.0)
            st_ref[p] = st * jnp.exp(g_end[:, ps]) + upd
            for hh in range(2):
                h = 2 * p + hh
                hs = slice(h * GLA_DV, (h + 1) * GLA_DV)
                scores = jnp.concatenate(sc[hh], axis=0)
                o = _dot(scores, v_ref[r0:r0 + c, hs]) + o_inter[:, hh * GLA_DV:(hh + 1) * GLA_DV]
                o = o * lax.rsqrt(jnp.mean(o * o, axis=-1, keepdims=True) + LN_EPS) * norm
                out_ref[r0:r0 + c, hs] = (o * g_ref[r0:r0 + c, hs].astype(F32)).astype(BF16)


def _gla_call(layer, bsz, seqlen, gqk, gv, gg, loga, pp):
    tl = SEQ_TILE
    nl = seqlen // tl
    row = lambda n: pl.BlockSpec((tl, n), lambda b, l: (b * nl + l, 0))
    return pl.pallas_call(
        _gla_kernel,
        grid=(bsz, nl),
        in_specs=[row(2 * GLA_QK), row(GLA_VW), row(GLA_VW), row(GLA_QK),
                  _const_spec((1, LANES), layer)],
        out_specs=row(GLA_VW),
        out_shape=jax.ShapeDtypeStruct((bsz * seqlen, GLA_VW), BF16),
        scratch_shapes=[pltpu.VMEM((GLA_HEADS // 2, 2 * GLA_DV, 2 * GLA_DK), F32)],
        compiler_params=_params(("arbitrary", "arbitrary")),
        name="gla",
    )(gqk, gv, gg, loga, pp["gla_norm"])


def _s5_kernel(u_ref, t_ref, pr_ref, pi_ref, qr_ref, qi_ref, ac_ref, y_ref,
               sr_ref, si_ref, xr_ref, xi_ref):
    bsz = SUBLANES
    u = u_ref[...]
    nsteps = u.shape[0] // bsz
    sr_ref[...] = jnp.dot(u, pr_ref[...], preferred_element_type=F32)
    si_ref[...] = jnp.dot(u, pi_ref[...], preferred_element_type=F32)
    ar = ac_ref[0:1, :]
    ai = ac_ref[1:2, :]

    def step(i, carry):
        xr, xi = carry
        r = pl.multiple_of(i * bsz, bsz)
        xr_ref[pl.ds(r, bsz), :] = xr
        xi_ref[pl.ds(r, bsz), :] = xi
        sr = sr_ref[pl.ds(r, bsz), :]
        si = si_ref[pl.ds(r, bsz), :]
        return ar * xr - ai * xi + sr, ar * xi + ai * xr + si

    zero = jnp.zeros((bsz, S5_STATE), F32)
    lax.fori_loop(0, nsteps, step, (zero, zero))
    y_ref[...] = (jnp.dot(u, t_ref[...], preferred_element_type=F32)
                  + _dot(xr_ref[...], qr_ref[...])
                  + _dot(xi_ref[...], qi_ref[...])).astype(BF16)


def _s5_call(layer, u, pp):
    g, r, w = u.shape
    blk = lambda a, b: pl.BlockSpec((None, None, a, b), lambda i: (layer, i, 0, 0))
    return pl.pallas_call(
        _s5_kernel,
        grid=(g,),
        in_specs=[pl.BlockSpec((None, r, w), lambda i: (i, 0, 0)),
                  blk(w, w), blk(w, S5_STATE), blk(w, S5_STATE), blk(S5_STATE, w), blk(S5_STATE, w),
                  blk(2, S5_STATE)],
        out_specs=pl.BlockSpec((None, r, w), lambda i: (i, 0, 0)),
        out_shape=jax.ShapeDtypeStruct((g, r, w), BF16),
        scratch_shapes=[pltpu.VMEM((r, S5_STATE), F32)] * 4,
        compiler_params=_params(("arbitrary",)),
        name="s5",
    )(u, pp["s5_t"], pp["s5_pr"], pp["s5_pi"], pp["s5_qr"], pp["s5_qi"], pp["s5_ac"])


CF_HALO = 32


def _merge_kernel(x_ref, a_ref, cf_ref, z_ref, d_ref, gates_ref,
                  wdn_ref, wcf_ref, ws5_ref, wgla_ref, wo_ref,
                  cfw_ref, cfb_ref, cfg_ref, cfbeta_ref, lng_ref, lnb_ref,
                  out_ref, ext_ref, conv_ref):
    tl = x_ref.shape[0]

    @pl.when(pl.program_id(1) == 0)
    def _():
        ext_ref[0:CF_HALO, :] = jnp.zeros((CF_HALO, CF_WIDTH), F32)

    ext_ref[CF_HALO:CF_HALO + tl, :] = cf_ref[...].astype(F32)
    base = CF_HALO - (CF_KERNEL - 1)
    for j in range(CF_WIDTH // LANES):
        sl = slice(j * LANES, (j + 1) * LANES)
        acc = cfw_ref[0:1, sl] * ext_ref[base:base + tl, sl]
        for k in range(1, CF_KERNEL):
            acc = acc + cfw_ref[k:k + 1, sl] * ext_ref[base + k:base + k + tl, sl]
        conv_ref[:, sl] = acc
    ext_ref[0:CF_HALO, :] = ext_ref[tl:tl + CF_HALO, :]

    cfo = _silu(_layer_norm(conv_ref[...] + cfb_ref[...], cfg_ref[...], cfbeta_ref[...]))
    y_b = jnp.dot(cfo.astype(BF16), wcf_ref[...], preferred_element_type=F32)
    y_a = jnp.dot(a_ref[...], wdn_ref[...], preferred_element_type=F32)
    y_d = jnp.dot(d_ref[...], wgla_ref[...], preferred_element_type=F32)
    zg = _gelu_tanh(z_ref[...].astype(F32)).astype(BF16)
    z_val = jnp.dot(zg, ws5_ref[:, 0:D_MODEL], preferred_element_type=F32)
    z_gate = jnp.dot(zg, ws5_ref[:, D_MODEL:2 * D_MODEL], preferred_element_type=F32)
    y_c = z_val * _sigmoid(z_gate)
    gt = lambda i: gates_ref[:, i * D_MODEL:(i + 1) * D_MODEL].astype(F32)
    merged = gt(0) * y_a + gt(1) * y_b + gt(2) * y_c + gt(3) * y_d
    mix = jnp.dot(merged.astype(BF16), wo_ref[...], preferred_element_type=F32)
    out_ref[...] = _layer_norm(DEEPNORM_ALPHA * x_ref[...] + mix, lng_ref[...], lnb_ref[...])


def _merge_call(layer, bsz, seqlen, x2, a_in, cf, z, d_in, gates, pp):
    tl = SEQ_TILE
    nl = seqlen // tl
    row = lambda n: pl.BlockSpec((tl, n), lambda b, l: (b * nl + l, 0))
    return pl.pallas_call(
        _merge_kernel,
        grid=(bsz, nl),
        in_specs=[row(D_MODEL), row(512), row(CF_WIDTH), row(S5_WIDTH), row(GLA_VW),
                  row(N_BRANCH * D_MODEL),
                  _const_spec((512, D_MODEL), layer), _const_spec((CF_WIDTH, D_MODEL), layer),
                  _const_spec((S5_WIDTH, 2 * D_MODEL), layer), _const_spec((GLA_VW, D_MODEL), layer),
                  _const_spec((D_MODEL, D_MODEL), layer),
                  _const_spec((CF_HALO, CF_WIDTH), layer), _const_spec((1, CF_WIDTH), layer),
                  _const_spec((1, CF_WIDTH), layer), _const_spec((1, CF_WIDTH), layer),
                  _const_spec((1, D_MODEL), layer), _const_spec((1, D_MODEL), layer)],
        out_specs=row(D_MODEL),
        out_shape=jax.ShapeDtypeStruct((bsz * seqlen, D_MODEL), F32),
        scratch_shapes=[pltpu.VMEM((tl + CF_HALO, CF_WIDTH), F32),
                        pltpu.VMEM((tl, CF_WIDTH), F32)],
        compiler_params=_params(("arbitrary", "arbitrary")),
        name="merge",
    )(x2, a_in, cf, z, d_in, gates, pp["w_br_dn"], pp["w_br_cf"], pp["w_br_s5"], pp["w_br_gla"],
      pp["w_o"], pp["cf_dw"], pp["cf_dw_bias"], pp["cf_ln_g"], pp["cf_ln_b"], pp["ln1_g"], pp["ln1_b"])


def _ffn_kernel(x_ref, wup_ref, convw_ref, wdown_ref, lng_ref, lnb_ref, out_ref,
                halo_ref, ext_ref):
    tl = x_ref.shape[0]
    w = FFN_COLS

    @pl.when(pl.program_id(1) == 0)
    def _():
        halo_ref[...] = jnp.zeros(halo_ref.shape, F32)

    x = x_ref[...]
    xb = x.astype(BF16)
    acc = jnp.zeros((tl, D_MODEL), F32)
    for ci in range(D_FF // w):
        halves = []
        for half in range(2):
            c0 = half * D_FF + ci * w
            cs = slice(c0, c0 + w)
            u = jnp.dot(xb, wup_ref[:, cs], preferred_element_type=F32)
            ext_ref[0:SUBLANES, :] = halo_ref[:, cs]
            ext_ref[SUBLANES:SUBLANES + tl, :] = u
            halo_ref[:, cs] = u[tl - SUBLANES:tl, :]
            y = convw_ref[FFN_CONV - 1:FFN_CONV, cs] * u
            for k in range(FFN_CONV - 1):
                off = SUBLANES - (FFN_CONV - 1) + k
                y = y + convw_ref[k:k + 1, cs] * ext_ref[off:off + tl, :]
            halves.append(y)
        hidden = (_silu(halves[0]) * halves[1]).astype(BF16)
        acc = acc + jnp.dot(hidden, wdown_ref[ci * w:(ci + 1) * w, :], preferred_element_type=F32)
    out_ref[...] = _layer_norm(DEEPNORM_ALPHA * x + acc, lng_ref[...], lnb_ref[...])


def _ffn_call(layer, bsz, seqlen, x2, pp):
    tl = SEQ_TILE
    nl = seqlen // tl
    row = lambda n: pl.BlockSpec((tl, n), lambda b, l: (b * nl + l, 0))
    return pl.pallas_call(
        _ffn_kernel,
        grid=(bsz, nl),
        in_specs=[row(D_MODEL),
                  _const_spec((D_MODEL, 2 * D_FF), layer), _const_spec((SUBLANES, 2 * D_FF), layer),
                  _const_spec((D_FF, D_MODEL), layer),
                  _const_spec((1, D_MODEL), layer), _const_spec((1, D_MODEL), layer)],
        out_specs=row(D_MODEL),
        out_shape=jax.ShapeDtypeStruct((bsz * seqlen, D_MODEL), F32),
        scratch_shapes=[pltpu.VMEM((SUBLANES, 2 * D_FF), F32),
                        pltpu.VMEM((tl + SUBLANES, FFN_COLS), F32)],
        compiler_params=_params(("arbitrary", "arbitrary")),
        name="convffn",
    )(x2, pp["w_up"], pp["ffn_conv"], pp["w_down"], pp["ln2_g"], pp["ln2_b"])


def _pad_rows(a, rows):
    return jnp.pad(a, ((0, 0), (0, rows - a.shape[1]), (0, 0)))


def _s5_tables(a_re, a_im, log_dt, b_re, b_im, c_re, c_im, d):
    c = S5_CHUNK
    dt = jnp.exp(log_dt)[..., None]
    mag = jnp.exp(dt * a_re)
    abar_re, abar_im = mag * jnp.cos(dt * a_im), mag * jnp.sin(dt * a_im)
    den = a_re * a_re + a_im * a_im
    nr, ni = abar_re - 1.0, abar_im
    fr, fi = (nr * a_re + ni * a_im) / den, (ni * a_re - nr * a_im) / den
    bb_re = fr[..., None] * b_re - fi[..., None] * b_im
    bb_im = fr[..., None] * b_im + fi[..., None] * b_re
    j = jnp.arange(c + 1, dtype=F32)[:, None, None, None]
    pmag = jnp.exp(j * (dt * a_re)[None])
    pw_re = pmag * jnp.cos(j * (dt * a_im)[None])
    pw_im = pmag * jnp.sin(j * (dt * a_im)[None])
    cb_re = jnp.einsum('lgon,lgni->lgnoi', c_re, bb_re) - jnp.einsum('lgon,lgni->lgnoi', c_im, bb_im)
    cb_im = jnp.einsum('lgon,lgni->lgnoi', c_re, bb_im) + jnp.einsum('lgon,lgni->lgnoi', c_im, bb_re)
    kern = (jnp.einsum('jlgn,lgnoi->jlgoi', pw_re[:c], cb_re)
            - jnp.einsum('jlgn,lgnoi->jlgoi', pw_im[:c], cb_im))
    eye = jnp.eye(S5_GROUP, dtype=F32)
    kern = kern.at[0].add(d.reshape(d.shape[0], S5_GROUPS, S5_GROUP)[..., None] * eye)
    s_idx = jnp.arange(c)[:, None]
    t_idx = jnp.arange(c)[None, :]
    lag = t_idx - s_idx
    toe = jnp.where((lag >= 0)[..., None, None, None, None], kern[jnp.clip(lag, 0, c - 1)], 0.0)
    toe = toe.transpose(2, 3, 0, 5, 1, 4).reshape(d.shape[0], S5_GROUPS, c * S5_GROUP, c * S5_GROUP)
    rp_re, rp_im = pw_re[:c][::-1], pw_im[:c][::-1]
    p_re = rp_re[..., None] * bb_re[None] - rp_im[..., None] * bb_im[None]
    p_im = rp_re[..., None] * bb_im[None] + rp_im[..., None] * bb_re[None]
    tos = lambda a: a.transpose(1, 2, 0, 4, 3).reshape(d.shape[0], S5_GROUPS, c * S5_GROUP, S5_STATE)
    q_re = (jnp.einsum('lgon,tlgn->lgnto', c_re, pw_re[1:]) - jnp.einsum('lgon,tlgn->lgnto', c_im, pw_im[1:]))
    q_im = -(jnp.einsum('lgon,tlgn->lgnto', c_re, pw_im[1:]) + jnp.einsum('lgon,tlgn->lgnto', c_im, pw_re[1:]))
    toq = lambda a: a.reshape(d.shape[0], S5_GROUPS, S5_STATE, c * S5_GROUP)
    ac = jnp.stack([pw_re[c], pw_im[c]], axis=2)
    return dict(s5_t=toe.astype(BF16), s5_pr=tos(p_re).astype(BF16), s5_pi=tos(p_im).astype(BF16),
                s5_qr=toq(q_re).astype(BF16), s5_qi=toq(q_im).astype(BF16), s5_ac=ac)


def _pack_params(w_in, dn_conv, dn_a_log, dn_dt_bias, dn_norm, w_br_dn, cf_dw, cf_dw_bias, cf_ln_g,
                 cf_ln_b, w_br_cf, s5_a_re, s5_a_im, s5_log_dt, s5_b_re, s5_b_im, s5_c_re, s5_c_im,
                 s5_d, w_br_s5, gla_w_alpha, gla_b_alpha, gla_norm, w_br_gla, w_o, ln1_g, ln1_b,
                 w_up, ffn_conv, w_down, ln2_g, ln2_b):
    nl = w_in.shape[0]
    sizes = (DN_QKV, DN_HEADS, DN_HEADS, 512, 2 * CF_WIDTH, S5_WIDTH, GLA_QK, GLA_QK, GLA_VW, GLA_VW,
             GLA_RANK, N_BRANCH * D_MODEL)
    offs = [0]
    for s in sizes:
        offs.append(offs[-1] + s)
    col = lambda i: w_in[:, :, offs[i]:offs[i + 1]]
    w_big = jnp.concatenate([col(0), col(3), col(4), col(5), col(6), col(7), col(8), col(9), col(11)],
                            axis=-1).astype(BF16)
    w_small = jnp.concatenate([col(1), col(2), col(10)], axis=-1)
    w_small = jnp.pad(w_small, ((0, 0), (0, 0), (0, LANES - w_small.shape[-1])))
    ws_hi = w_small.astype(BF16)
    ws_lo = (w_small - ws_hi.astype(F32)).astype(BF16)
    pad_l = lambda a: jnp.pad(a, ((0, 0), (0, LANES - a.shape[-1])))
    dn_vec = jnp.stack([pad_l(-jnp.exp(dn_a_log)), pad_l(dn_dt_bias)], axis=1)
    dn_vec = _pad_rows(dn_vec, SUBLANES)
    wal = jnp.pad(gla_w_alpha, ((0, 0), (2 * DN_HEADS, LANES - 2 * DN_HEADS - GLA_RANK), (0, 0)))
    wal_hi = wal.astype(BF16)
    wal_lo = (wal - wal_hi.astype(F32)).astype(BF16)
    vec = lambda a: a[:, None, :]
    pp = dict(
        w_big=w_big, ws_hi=ws_hi, ws_lo=ws_lo, dn_vec=dn_vec, wal_hi=wal_hi, wal_lo=wal_lo,
        b_alpha=vec(gla_b_alpha),
        dn_conv=dn_conv, dn_norm=vec(dn_norm), gla_norm=vec(gla_norm),
        w_br_dn=w_br_dn.astype(BF16), w_br_cf=w_br_cf.astype(BF16), w_br_s5=w_br_s5.astype(BF16),
        w_br_gla=w_br_gla.astype(BF16), w_o=w_o.astype(BF16),
        cf_dw=_pad_rows(cf_dw, CF_HALO), cf_dw_bias=vec(cf_dw_bias), cf_ln_g=vec(cf_ln_g),
        cf_ln_b=vec(cf_ln_b), ln1_g=vec(ln1_g), ln1_b=vec(ln1_b),
        w_up=w_up.astype(BF16), ffn_conv=_pad_rows(ffn_conv, SUBLANES), w_down=w_down.astype(BF16),
        ln2_g=vec(ln2_g), ln2_b=vec(ln2_b),
    )
    pp.update(_s5_tables(s5_a_re, s5_a_im, s5_log_dt, s5_b_re, s5_b_im, s5_c_re, s5_c_im, s5_d))
    del nl
    return pp


def kernel(x, w_in, dn_conv, dn_a_log, dn_dt_bias, dn_norm, w_br_dn, cf_dw, cf_dw_bias, cf_ln_g, cf_ln_b, w_br_cf, s5_a_re, s5_a_im, s5_log_dt, s5_b_re, s5_b_im, s5_c_re, s5_c_im, s5_d, w_br_s5, gla_w_alpha, gla_b_alpha, gla_norm, w_br_gla, w_o, ln1_g, ln1_b, w_up, ffn_conv, w_down, ln2_g, ln2_b):
    bsz, seqlen, d_model = x.shape
    assert d_model == D_MODEL and bsz == SUBLANES
    assert seqlen % SEQ_TILE == 0 and (bsz * seqlen) % PROJ_TILE == 0
    depth = w_in.shape[0]
    pp = _pack_params(w_in, dn_conv, dn_a_log, dn_dt_bias, dn_norm, w_br_dn, cf_dw, cf_dw_bias,
                      cf_ln_g, cf_ln_b, w_br_cf, s5_a_re, s5_a_im, s5_log_dt, s5_b_re, s5_b_im,
                      s5_c_re, s5_c_im, s5_d, w_br_s5, gla_w_alpha, gla_b_alpha, gla_norm, w_br_gla,
                      w_o, ln1_g, ln1_b, w_up, ffn_conv, w_down, ln2_g, ln2_b)
    t = bsz * seqlen
    nch = seqlen // S5_CHUNK
    x2 = x.reshape(t, D_MODEL)
    for layer in range(depth):
        (qkv, dng, cf, s5_in, gqk, gv, gg, gates, small, loga) = _proj_call(layer, x2, pp)
        smallt = small[:, :SUBLANES].reshape(t // DN_CHUNK, DN_CHUNK, SUBLANES).transpose(0, 2, 1)
        a_in = _dn_call(layer, bsz, seqlen, qkv, small, smallt, dng, pp)
        d_in = _gla_call(layer, bsz, seqlen, gqk, gv, gg, loga, pp)
        u = s5_in.reshape(bsz, nch, S5_CHUNK, S5_GROUPS, S5_GROUP).transpose(3, 1, 0, 2, 4)
        u = u.reshape(S5_GROUPS, nch * bsz, S5_CHUNK * S5_GROUP)
        y = _s5_call(layer, u, pp)
        z = y.reshape(S5_GROUPS, nch, bsz, S5_CHUNK, S5_GROUP).transpose(2, 1, 3, 0, 4)
        z = z.reshape(t, S5_WIDTH)
        x2 = _merge_call(layer, bsz, seqlen, x2, a_in, cf, z, d_in, gates, pp)
        x2 = _ffn_call(layer, bsz, seqlen, x2, pp)
    return x2.reshape(bsz, seqlen, D_MODEL)
```

```python
import functools
import math

import jax
import jax.numpy as jnp
from jax import lax
from jax.experimental import pallas as pl
from jax.experimental.pallas import tpu as pltpu

F32 = jnp.float32
BF16 = jnp.bfloat16

D_MODEL = 1024
DN_HEADS = 4
DN_DK = 128
DN_CONV = 4
DN_CHUNK = 64
DN_QKV = 1536
CF_WIDTH = 512
CF_KERNEL = 31
S5_WIDTH = 512
S5_GROUP = 16
S5_GROUPS = 32
S5_STATE = 64
S5_CHUNK = 32
GLA_HEADS = 4
GLA_DK = 64
GLA_DV = 128
GLA_QK = 256
GLA_VW = 512
GLA_RANK = 16
GLA_TAU = 16.0
GLA_CHUNK = 64
GLA_SUB = 16
N_BRANCH = 4
D_FF = 2816
FFN_CONV = 3
FFN_COLS = 256
LN_EPS = 1e-5
DEPTH = 4
DEEPNORM_ALPHA = (2.0 * DEPTH) ** 0.25

LANES = 128
SUBLANES = 8
SEQ_TILE = 256
MERGE_TILE = 256
FFN_TILE = 256
PROJ_TILE = 256
VMEM_LIMIT = 56 * 1024 * 1024

C_DNG = 0
C_CF = 512
C_S5 = 1536
C_GQK = 2048
C_GV = 2560
C_GG = 3072
W_MID = 3584


def _dot(a, b):
    return jnp.dot(a.astype(BF16), b.astype(BF16), preferred_element_type=F32)


def _dot_nt(a, b):
    return lax.dot_general(a.astype(BF16), b.astype(BF16), (((1,), (1,)), ((), ())),
                           preferred_element_type=F32)


def _dot_tn(a, b):
    return lax.dot_general(a.astype(BF16), b.astype(BF16), (((0,), (0,)), ((), ())),
                           preferred_element_type=F32)


def _split3(x):
    x1 = x.astype(BF16)
    r = x - x1.astype(F32)
    x2 = r.astype(BF16)
    x3 = (r - x2.astype(F32)).astype(BF16)
    return x1, x2, x3


def _split2(x):
    x1 = x.astype(BF16)
    x2 = (x - x1.astype(F32)).astype(BF16)
    return x1, x2


def _dot01_left(m01, x):
    x1, x2, x3 = _split3(x)
    d = lambda v: jnp.dot(m01, v, preferred_element_type=F32)
    return d(x3) + d(x2) + d(x1)


def _dot01_right(x, m01):
    x1, x2, x3 = _split3(x)
    d = lambda v: jnp.dot(v, m01, preferred_element_type=F32)
    return d(x3) + d(x2) + d(x1)


def _dot_hilo(a, b_hi, b_lo):
    a_hi, a_lo = _split2(a)
    d = lambda u, v: jnp.dot(u, v, preferred_element_type=F32)
    return d(a_lo, b_hi) + d(a_hi, b_lo) + d(a_hi, b_hi)


def _sigmoid(x):
    return 1.0 / (1.0 + jnp.exp(-x))


def _silu(x):
    return x * _sigmoid(x)


def _softplus(x):
    return jnp.maximum(x, 0.0) + jnp.log(1.0 + jnp.exp(-jnp.abs(x)))


def _gelu_tanh(x):
    c = math.sqrt(2.0 / math.pi)
    return 0.5 * x * (1.0 + jnp.tanh(c * (x + 0.044715 * (x * x * x))))


def _layer_norm(x, g, b):
    mu = jnp.mean(x, axis=-1, keepdims=True)
    xc = x - mu
    var = jnp.mean(xc * xc, axis=-1, keepdims=True)
    return xc * lax.rsqrt(var + LN_EPS) * g + b


def _tri(n, kind):
    r = lax.broadcasted_iota(jnp.int32, (n, n), 0)
    c = lax.broadcasted_iota(jnp.int32, (n, n), 1)
    if kind == "lower":
        return r >= c
    if kind == "strict":
        return r > c
    if kind == "upper":
        return r <= c
    raise ValueError(kind)


def _const_spec(shape, layer):
    nd = len(shape)
    return pl.BlockSpec((None,) + tuple(shape), lambda *_: (layer,) + (0,) * nd,
                        pipeline_mode=pl.Buffered(1))


def _params(sem):
    return pltpu.CompilerParams(dimension_semantics=sem, vmem_limit_bytes=VMEM_LIMIT)


def _proj_kernel(x_ref, wq_ref, w_ref, wg_ref, wsh_ref, wsl_ref, vec_ref, walh_ref, wall_ref, bal_ref,
                 qkv_ref, dng_ref, cf_ref, s5_ref, gqk_ref, gv_ref, gg_ref, gates_ref,
                 small_ref, loga_ref):
    x = x_ref[...]
    xb = x.astype(BF16)

    def mm(c0, n):
        return jnp.dot(xb, w_ref[:, c0:c0 + n], preferred_element_type=F32)

    for j in range(DN_QKV // 512):
        qkv_ref[:, j * 512:(j + 1) * 512] = jnp.dot(
            xb, wq_ref[:, j * 512:(j + 1) * 512], preferred_element_type=F32).astype(BF16)
    dng_ref[...] = _silu(mm(C_DNG, 512)).astype(BF16)
    cf_a = mm(C_CF, 512)
    cf_g = mm(C_CF + 512, 512)
    cf_ref[...] = (cf_a * _sigmoid(cf_g)).astype(BF16)
    s5_ref[...] = mm(C_S5, 512).astype(BF16)
    gqk_ref[:, 0:GLA_QK] = (mm(C_GQK, GLA_QK) * (GLA_DK ** -0.5)).astype(BF16)
    gqk_ref[:, GLA_QK:2 * GLA_QK] = mm(C_GQK + GLA_QK, GLA_QK).astype(BF16)
    gv_ref[...] = mm(C_GV, 512).astype(BF16)
    gg_ref[...] = _silu(mm(C_GG, 512)).astype(BF16)
    for j in range(N_BRANCH * D_MODEL // 512):
        gates_ref[:, j * 512:(j + 1) * 512] = _sigmoid(jnp.dot(
            xb, wg_ref[:, j * 512:(j + 1) * 512], preferred_element_type=F32)).astype(BF16)

    x_lo = (x - xb.astype(F32)).astype(BF16)
    d = lambda u, v: jnp.dot(u, v, preferred_element_type=F32)
    s = d(x_lo, wsh_ref[...]) + d(xb, wsl_ref[...]) + d(xb, wsh_ref[...])
    lane = lax.broadcasted_iota(jnp.int32, s.shape, 1)
    neg_a = vec_ref[0:1, :]
    dt_b = vec_ref[1:2, :]
    g = neg_a * _softplus(s + dt_b)
    beta = _sigmoid(s)
    small = jnp.where(lane < DN_HEADS, g, jnp.where(lane < 2 * DN_HEADS, beta, s))
    small_ref[...] = small
    z = _dot_hilo(s, walh_ref[...], wall_ref[...]) + bal_ref[...]
    log_sig = jnp.minimum(z, 0.0) - jnp.log(1.0 + jnp.exp(-jnp.abs(z)))
    loga_ref[...] = log_sig * (1.0 / GLA_TAU)


def _proj_call(layer, x2, pp):
    t = x2.shape[0]
    tm = PROJ_TILE
    row = lambda n: pl.BlockSpec((tm, n), lambda i: (i, 0))
    outs = [(DN_QKV, BF16), (512, BF16), (512, BF16), (512, BF16), (512, BF16), (512, BF16),
            (512, BF16), (N_BRANCH * D_MODEL, BF16), (LANES, F32), (GLA_QK, F32)]
    return pl.pallas_call(
        _proj_kernel,
        grid=(t // tm,),
        in_specs=[row(D_MODEL),
                  _const_spec((D_MODEL, DN_QKV), layer),
                  _const_spec((D_MODEL, W_MID), layer),
                  _const_spec((D_MODEL, N_BRANCH * D_MODEL), layer),
                  _const_spec((D_MODEL, LANES), layer),
                  _const_spec((D_MODEL, LANES), layer),
                  _const_spec((SUBLANES, LANES), layer),
                  _const_spec((LANES, GLA_QK), layer),
                  _const_spec((LANES, GLA_QK), layer),
                  _const_spec((1, GLA_QK), layer)],
        out_specs=[row(n) for n, _ in outs],
        out_shape=[jax.ShapeDtypeStruct((t, n), dt) for n, dt in outs],
        compiler_params=_params(("arbitrary",)),
        name="proj",
    )(x2, pp["w_qkv"], pp["w_mid"], pp["w_gate"], pp["ws_hi"], pp["ws_lo"], pp["dn_vec"], pp["wal_hi"], pp["wal_lo"],
      pp["b_alpha"])


def _dn_kernel(qkv_ref, small_ref, smallt_ref, gate_ref, convw_ref, norm_ref, out_ref,
               ext_ref, qkvc_ref, st_ref):
    tl = qkv_ref.shape[0]
    c = DN_CHUNK

    @pl.when(pl.program_id(1) == 0)
    def _():
        ext_ref[0:SUBLANES, :] = jnp.zeros((SUBLANES, DN_QKV), F32)
        st_ref[...] = jnp.zeros(st_ref.shape, F32)

    ext_ref[SUBLANES:SUBLANES + tl, :] = qkv_ref[...].astype(F32)
    for j in range(DN_QKV // LANES):
        sl = slice(j * LANES, (j + 1) * LANES)
        acc = convw_ref[DN_CONV - 1:DN_CONV, sl] * ext_ref[SUBLANES:SUBLANES + tl, sl]
        for k in range(DN_CONV - 1):
            off = SUBLANES - (DN_CONV - 1) + k
            acc = acc + convw_ref[k:k + 1, sl] * ext_ref[off:off + tl, sl]
        y = _silu(acc)
        if j < 2 * DN_HEADS:
            y = y * lax.rsqrt(jnp.sum(y * y, axis=-1, keepdims=True) + 1e-6)
            if j < DN_HEADS:
                y = y * (DN_DK ** -0.5)
        qkvc_ref[:, sl] = y
    ext_ref[0:SUBLANES, :] = ext_ref[tl:tl + SUBLANES, :]

    causal = _tri(c, "lower")
    strict = _tri(c, "strict")
    tril01 = causal.astype(BF16)
    triu01 = _tri(c, "upper").astype(BF16)
    eye = (lax.broadcasted_iota(jnp.int32, (c, c), 0)
           == lax.broadcasted_iota(jnp.int32, (c, c), 1)).astype(F32)
    norm = norm_ref[...]

    items = [(ci, h) for ci in range(tl // c) for h in range(DN_HEADS)]
    ms, rhs, amat, qds, kds, lasts = [], [], [], [], [], []
    for ci in range(tl // c):
        r0 = ci * c
        sm = small_ref[r0:r0 + c, :]
        gc_all = _dot01_left(tril01, sm)
        gcr_all = _dot01_right(smallt_ref[ci], triu01)
        exp_gc = jnp.exp(gc_all)
        g_last = gc_all[c - 1:c, :]
        exp_rem = jnp.exp(g_last - gc_all)
        exp_last = jnp.exp(g_last)
        for h in range(DN_HEADS):
            q = qkvc_ref[r0:r0 + c, h * LANES:(h + 1) * LANES]
            k = qkvc_ref[r0:r0 + c, (4 + h) * LANES:(5 + h) * LANES]
            v = qkvc_ref[r0:r0 + c, (8 + h) * LANES:(9 + h) * LANES]
            beta = sm[:, DN_HEADS + h:DN_HEADS + h + 1]
            diff = gc_all[:, h:h + 1] - gcr_all[h:h + 1, :]
            decay = jnp.where(causal, jnp.exp(jnp.where(causal, diff, 0.0)), 0.0)
            kb = k * beta
            kbf = k.astype(BF16)
            ms.append(-jnp.where(strict, _dot_nt(kb, kbf) * decay, 0.0))
            amat.append((_dot_nt(q, kbf) * decay).astype(BF16))
            rhs.append(jnp.concatenate([v * beta, kb * exp_gc[:, h:h + 1]], axis=1).astype(BF16))
            qds.append(q * exp_gc[:, h:h + 1])
            kds.append((k * exp_rem[:, h:h + 1]).astype(BF16))
            lasts.append(exp_last[:, h:h + 1])
    ps = [eye + m for m in ms]
    for _ in range(int(math.log2(c)) - 1):
        ms = [_dot(m, m) for m in ms]
        ps = [p + _dot(p, m) for p, m in zip(ps, ms)]
    sols = [_dot(p, r) for p, r in zip(ps, rhs)]
    us = [s[:, :LANES].astype(BF16) for s in sols]
    ws = [s[:, LANES:].astype(BF16) for s in sols]
    qeff = [qd - _dot(a, w) for qd, a, w in zip(qds, amat, ws)]
    oc = [_dot(a, u) for a, u in zip(amat, us)]
    gmat = [_dot_tn(kd, w) for kd, w in zip(kds, ws)]
    bmat = [_dot_tn(kd, u) for kd, u in zip(kds, us)]
    for idx, (ci, h) in enumerate(items):
        r0 = ci * c
        hs = slice(h * LANES, (h + 1) * LANES)
        s = st_ref[h]
        sb = s.astype(BF16)
        o = _dot(qeff[idx], sb) + oc[idx]
        st_ref[h] = s * lasts[idx] - _dot(gmat[idx], sb) + bmat[idx]
        o = o * lax.rsqrt(jnp.mean(o * o, axis=-1, keepdims=True) + LN_EPS) * norm
        out_ref[r0:r0 + c, hs] = (o * gate_ref[r0:r0 + c, hs].astype(F32)).astype(BF16)


def _dn_call(layer, bsz, seqlen, qkv, small, smallt, gate, pp):
    tl = SEQ_TILE
    nl = seqlen // tl
    row = lambda n: pl.BlockSpec((tl, n), lambda b, l: (b * nl + l, 0))
    return pl.pallas_call(
        _dn_kernel,
        grid=(bsz, nl),
        in_specs=[row(DN_QKV), row(LANES),
                  pl.BlockSpec((tl // DN_CHUNK, SUBLANES, DN_CHUNK), lambda b, l: (b * nl + l, 0, 0)),
                  row(512),
                  _const_spec((DN_CONV, DN_QKV), layer),
                  _const_spec((1, LANES), layer)],
        out_specs=row(512),
        out_shape=jax.ShapeDtypeStruct((bsz * seqlen, 512), BF16),
        scratch_shapes=[pltpu.VMEM((tl + SUBLANES, DN_QKV), F32),
                        pltpu.VMEM((tl, DN_QKV), F32),
                        pltpu.VMEM((DN_HEADS, DN_DK, LANES), F32)],
        compiler_params=_params(("arbitrary", "arbitrary")),
        name="deltanet",
    )(qkv, small, smallt, gate, pp["dn_conv"], pp["dn_norm"])


def _gla_kernel(qk_ref, v_ref, g_ref, loga_ref, norm_ref, out_ref, st_ref):
    tl = qk_ref.shape[0]
    c = GLA_CHUNK
    nsub = c // GLA_SUB

    @pl.when(pl.program_id(1) == 0)
    def _():
        st_ref[...] = jnp.zeros(st_ref.shape, F32)

    tril01 = _tri(c, "lower").astype(BF16)
    rows = lax.broadcasted_iota(jnp.int32, (c, LANES), 0)
    lane = lax.broadcasted_iota(jnp.int32, (GLA_SUB, LANES), 1)
    srow = lax.broadcasted_iota(jnp.int32, (GLA_SUB, c), 0)
    scol = lax.broadcasted_iota(jnp.int32, (GLA_SUB, c), 1)
    st_r = lax.broadcasted_iota(jnp.int32, (2 * GLA_DV, LANES), 0) >= GLA_DV
    st_c = lax.broadcasted_iota(jnp.int32, (2 * GLA_DV, LANES), 1) >= GLA_DK
    st_mask = st_r == st_c
    norm = norm_ref[...]

    work = []
    for ci in range(tl // c):
        r0 = ci * c
        gcum = _dot01_left(tril01, loga_ref[r0:r0 + c, :])
        q = qk_ref[r0:r0 + c, 0:GLA_QK].astype(F32)
        k = qk_ref[r0:r0 + c, GLA_QK:2 * GLA_QK].astype(F32)
        g_end = gcum[c - 1:c, :]
        refs = [jnp.zeros((1, GLA_QK), F32)] + [gcum[GLA_SUB * i - 1:GLA_SUB * i, :] for i in range(1, nsub)]
        ref_rows = jnp.concatenate([jnp.broadcast_to(r, (GLA_SUB, GLA_QK)) for r in refs], axis=0)
        qn = q * jnp.exp(gcum - ref_rows)
        qdec = q * jnp.exp(gcum)
        kdec = k * jnp.exp(g_end - gcum)
        for p in range(GLA_HEADS // 2):
            ps = slice(p * LANES, (p + 1) * LANES)
            kp = k[:, ps]
            gp = gcum[:, ps]
            sc = [[], []]
            for i in range(nsub):
                e = jnp.where(rows < GLA_SUB * (i + 1), refs[i][:, ps] - gp, 0.0)
                kn = (kp * jnp.exp(e)).astype(BF16)
                qi = qn[GLA_SUB * i:GLA_SUB * (i + 1), ps]
                for hh in range(2):
                    lhs = jnp.where((lane >= GLA_DK) if hh else (lane < GLA_DK), qi, 0.0)
                    s = _dot_nt(lhs, kn)
                    sc[hh].append(jnp.where(scol <= srow + GLA_SUB * i, s, 0.0))
            vp = v_ref[r0:r0 + c, p * 2 * GLA_DV:(p + 1) * 2 * GLA_DV]
            upd = jnp.where(st_mask, _dot_tn(vp, kdec[:, ps]), 0.0)
            intra = [_dot(jnp.concatenate(sc[hh], axis=0),
                          v_ref[r0:r0 + c, (2 * p + hh) * GLA_DV:(2 * p + hh + 1) * GLA_DV])
                     for hh in range(2)]
            work.append((r0, p, qdec[:, ps].astype(BF16), upd, jnp.exp(g_end[:, ps]), intra))

    for r0, p, qd, upd, dec, intra in work:
        st = st_ref[p]
        o_inter = _dot_nt(qd, st)
        st_ref[p] = st * dec + upd
        for hh in range(2):
            hs = slice((2 * p + hh) * GLA_DV, (2 * p + hh + 1) * GLA_DV)
            o = intra[hh] + o_inter[:, hh * GLA_DV:(hh + 1) * GLA_DV]
            o = o * lax.rsqrt(jnp.mean(o * o, axis=-1, keepdims=True) + LN_EPS) * norm
            out_ref[r0:r0 + c, hs] = (o * g_ref[r0:r0 + c, hs].astype(F32)).astype(BF16)


def _gla_call(layer, bsz, seqlen, gqk, gv, gg, loga, pp):
    tl = SEQ_TILE
    nl = seqlen // tl
    row = lambda n: pl.BlockSpec((tl, n), lambda b, l: (b * nl + l, 0))
    return pl.pallas_call(
        _gla_kernel,
        grid=(bsz, nl),
        in_specs=[row(2 * GLA_QK), row(GLA_VW), row(GLA_VW), row(GLA_QK),
                  _const_spec((1, LANES), layer)],
        out_specs=row(GLA_VW),
        out_shape=jax.ShapeDtypeStruct((bsz * seqlen, GLA_VW), BF16),
        scratch_shapes=[pltpu.VMEM((GLA_HEADS // 2, 2 * GLA_DV, 2 * GLA_DK), F32)],
        compiler_params=_params(("arbitrary", "arbitrary")),
        name="gla",
    )(gqk, gv, gg, loga, pp["gla_norm"])


S5_GPT = LANES // S5_GROUP


def _s5_kernel(u_ref, kc_ref, ptr_ref, pti_ref, qr_ref, qi_ref, ac_ref, y_ref,
               xs_ref, ug_ref, yg_ref, ys_ref, t_ref):
    c = S5_CHUNK
    nch = u_ref.shape[0] // c
    xs_ref[...] = u_ref[...].astype(F32)
    lane = lax.broadcasted_iota(jnp.int32, (nch, LANES), 1)
    piece = [(lane >= S5_GROUP * i) & (lane < S5_GROUP * (i + 1)) for i in range(S5_GPT)]

    for q in range(c // S5_GPT):
        tiles = [None] * S5_GPT
        for i in range(S5_GPT):
            r = xs_ref[pl.ds(S5_GPT * q + i, nch, stride=c), :]
            for g in range(S5_GPT):
                sh = (S5_GROUP * (i - g)) % LANES
                rr = pltpu.roll(r, sh, axis=1) if sh else r
                tiles[g] = jnp.where(piece[i], rr, 0.0 if tiles[g] is None else tiles[g])
        for g in range(S5_GPT):
            ug_ref[g, :, LANES * q:LANES * (q + 1)] = tiles[g].astype(BF16)

    klane = lax.broadcasted_iota(jnp.int32, (S5_GROUP, c * S5_GROUP), 1)
    clane = lax.broadcasted_iota(jnp.int32, (S5_STATE, nch), 1)
    xrs, xis = [], []
    for g in range(S5_GPT):
        kc = kc_ref[g]
        for s in range(c):
            blk = kc if s == 0 else jnp.where(klane >= S5_GROUP * s,
                                              pltpu.roll(kc, S5_GROUP * s, axis=1), 0.0)
            t_ref[g, S5_GROUP * s:S5_GROUP * (s + 1), :] = blk.astype(BF16)
        u = ug_ref[g]
        yg_ref[g] = jnp.dot(u, t_ref[g], preferred_element_type=F32)
        xrs.append(_dot_nt(ptr_ref[g], u))
        xis.append(_dot_nt(pti_ref[g], u))
    prs = [ac_ref[g][:, 0:1] for g in range(S5_GPT)]
    pis = [ac_ref[g][:, 1:2] for g in range(S5_GPT)]
    dist = 1
    while dist < nch:
        for g in range(S5_GPT):
            sr = jnp.where(clane >= dist, pltpu.roll(xrs[g], dist, axis=1), 0.0)
            si = jnp.where(clane >= dist, pltpu.roll(xis[g], dist, axis=1), 0.0)
            pr, pi = prs[g], pis[g]
            xrs[g], xis[g] = xrs[g] + pr * sr - pi * si, xis[g] + pr * si + pi * sr
            prs[g], pis[g] = pr * pr - pi * pi, 2.0 * pr * pi
        dist *= 2
    for g in range(S5_GPT):
        xr = jnp.where(clane >= 1, pltpu.roll(xrs[g], 1, axis=1), 0.0)
        xi = jnp.where(clane >= 1, pltpu.roll(xis[g], 1, axis=1), 0.0)
        yg_ref[g] = yg_ref[g] + _dot_tn(xr, qr_ref[g]) + _dot_tn(xi, qi_ref[g])

    for s in range(c):
        q, i = divmod(s, S5_GPT)
        tile = None
        for g in range(S5_GPT):
            r = yg_ref[g, :, LANES * q:LANES * (q + 1)]
            sh = (S5_GROUP * (g - i)) % LANES
            if sh:
                r = pltpu.roll(r, sh, axis=1)
            tile = jnp.where(piece[g], r, 0.0 if tile is None else tile)
        ys_ref[pl.ds(s, nch, stride=c), :] = tile
    y_ref[...] = ys_ref[...].astype(BF16)


def _s5_call(layer, bsz, u2, pp):
    c = S5_CHUNK
    seqlen = u2.shape[0] // bsz
    nch = seqlen // c
    assert nch <= LANES, "the chunk scan keeps one sequence's chunks inside one lane tile"
    tab = lambda a, b: pl.BlockSpec((None, S5_GPT, a, b), lambda bi, j: (layer, j, 0, 0))
    io = pl.BlockSpec((seqlen, LANES), lambda bi, j: (bi, j))
    return pl.pallas_call(
        _s5_kernel,
        grid=(bsz, u2.shape[1] // LANES),
        in_specs=[io, tab(S5_GROUP, c * S5_GROUP), tab(S5_STATE, c * S5_GROUP),
                  tab(S5_STATE, c * S5_GROUP), tab(S5_STATE, c * S5_GROUP),
                  tab(S5_STATE, c * S5_GROUP), tab(S5_STATE, LANES)],
        out_specs=io,
        out_shape=jax.ShapeDtypeStruct(u2.shape, BF16),
        scratch_shapes=[pltpu.VMEM((seqlen, LANES), F32),
                        pltpu.VMEM((S5_GPT, nch, c * S5_GROUP), BF16),
                        pltpu.VMEM((S5_GPT, nch, c * S5_GROUP), F32),
                        pltpu.VMEM((seqlen, LANES), F32),
                        pltpu.VMEM((S5_GPT, c * S5_GROUP, c * S5_GROUP), BF16)],
        compiler_params=_params(("arbitrary", "arbitrary")),
        name="s5",
    )(u2, pp["s5_kc"], pp["s5_ptr"], pp["s5_pti"], pp["s5_qr"], pp["s5_qi"], pp["s5_ac"])


CF_HALO = 32


def _merge_kernel(x_ref, a_ref, cf_ref, z_ref, d_ref, gates_ref,
                  wdn_ref, wcf_ref, ws5_ref, wgla_ref, wo_ref,
                  cfw_ref, cfb_ref, cfg_ref, cfbeta_ref, lng_ref, lnb_ref,
                  out_ref, rot_ref, conv_ref):
    tl = x_ref.shape[0]

    @pl.when(pl.program_id(1) == 0)
    def _():
        rot_ref[0, 0:CF_HALO, :] = jnp.zeros((CF_HALO, CF_WIDTH), F32)

    y_a = jnp.dot(a_ref[...], wdn_ref[...], preferred_element_type=F32)
    y_d = jnp.dot(d_ref[...], wgla_ref[...], preferred_element_type=F32)
    zg = _gelu_tanh(z_ref[...].astype(F32)).astype(BF16)
    z_val = jnp.dot(zg, ws5_ref[:, 0:D_MODEL], preferred_element_type=F32)
    z_gate = jnp.dot(zg, ws5_ref[:, D_MODEL:2 * D_MODEL], preferred_element_type=F32)

    rot_ref[0, CF_HALO:CF_HALO + tl, :] = cf_ref[...].astype(F32)
    nrows = tl + CF_HALO - SUBLANES
    for r in range(1, SUBLANES):
        rot_ref[r, 0:nrows, :] = rot_ref[0, r:r + nrows, :]
    base = CF_HALO - (CF_KERNEL - 1)
    for j in range(CF_WIDTH // LANES):
        sl = slice(j * LANES, (j + 1) * LANES)
        acc = None
        for k in range(CF_KERNEL):
            r = (base + k) % SUBLANES
            q = base + k - r
            term = cfw_ref[k:k + 1, sl] * rot_ref[r, q:q + tl, sl]
            acc = term if acc is None else acc + term
        conv_ref[:, sl] = acc
    rot_ref[0, 0:CF_HALO, :] = rot_ref[0, tl:tl + CF_HALO, :]

    cfo = _silu(_layer_norm(conv_ref[...] + cfb_ref[...], cfg_ref[...], cfbeta_ref[...]))
    y_b = jnp.dot(cfo.astype(BF16), wcf_ref[...], preferred_element_type=F32)
    y_c = z_val * _sigmoid(z_gate)
    gt = lambda i: gates_ref[:, i * D_MODEL:(i + 1) * D_MODEL].astype(F32)
    merged = gt(0) * y_a + gt(1) * y_b + gt(2) * y_c + gt(3) * y_d
    mix = jnp.dot(merged.astype(BF16), wo_ref[...], preferred_element_type=F32)
    out_ref[...] = _layer_norm(DEEPNORM_ALPHA * x_ref[...] + mix, lng_ref[...], lnb_ref[...])


def _merge_call(layer, bsz, seqlen, x2, a_in, cf, z, d_in, gates, pp):
    tl = MERGE_TILE
    nl = seqlen // tl
    row = lambda n: pl.BlockSpec((tl, n), lambda b, l: (b * nl + l, 0))
    return pl.pallas_call(
        _merge_kernel,
        grid=(bsz, nl),
        in_specs=[row(D_MODEL), row(512), row(CF_WIDTH), row(S5_WIDTH), row(GLA_VW),
                  row(N_BRANCH * D_MODEL),
                  _const_spec((512, D_MODEL), layer), _const_spec((CF_WIDTH, D_MODEL), layer),
                  _const_spec((S5_WIDTH, 2 * D_MODEL), layer), _const_spec((GLA_VW, D_MODEL), layer),
                  _const_spec((D_MODEL, D_MODEL), layer),
                  _const_spec((CF_HALO, CF_WIDTH), layer), _const_spec((1, CF_WIDTH), layer),
                  _const_spec((1, CF_WIDTH), layer), _const_spec((1, CF_WIDTH), layer),
                  _const_spec((1, D_MODEL), layer), _const_spec((1, D_MODEL), layer)],
        out_specs=row(D_MODEL),
        out_shape=jax.ShapeDtypeStruct((bsz * seqlen, D_MODEL), F32),
        scratch_shapes=[pltpu.VMEM((SUBLANES, tl + CF_HALO, CF_WIDTH), F32),
                        pltpu.VMEM((tl, CF_WIDTH), F32)],
        compiler_params=_params(("arbitrary", "arbitrary")),
        name="merge",
    )(x2, a_in, cf, z, d_in, gates, pp["w_br_dn"], pp["w_br_cf"], pp["w_br_s5"], pp["w_br_gla"],
      pp["w_o"], pp["cf_dw"], pp["cf_dw_bias"], pp["cf_ln_g"], pp["cf_ln_b"], pp["ln1_g"], pp["ln1_b"])


def _ffn_kernel(x_ref, wup_ref, convw_ref, wdown_ref, lng_ref, lnb_ref, out_ref, ext_ref):
    tl = x_ref.shape[0]
    w = FFN_COLS

    @pl.when(pl.program_id(1) == 0)
    def _():
        ext_ref[0:SUBLANES, :] = jnp.zeros((SUBLANES, 2 * D_FF), F32)

    x = x_ref[...]
    xb = x.astype(BF16)
    nchunks = D_FF // w
    cols = lambda ci, half: slice(half * D_FF + ci * w, half * D_FF + (ci + 1) * w)

    def up(ci):
        us = []
        for half in range(2):
            u = jnp.dot(xb, wup_ref[:, cols(ci, half)], preferred_element_type=F32)
            ext_ref[SUBLANES:SUBLANES + tl, cols(ci, half)] = u
            us.append(u)
        return us

    acc = jnp.zeros((tl, D_MODEL), F32)
    us = up(0)
    for ci in range(nchunks):
        nxt = up(ci + 1) if ci + 1 < nchunks else None
        halves = []
        for half in range(2):
            cs = cols(ci, half)
            y = convw_ref[FFN_CONV - 1:FFN_CONV, cs] * us[half]
            for k in range(FFN_CONV - 1):
                off = SUBLANES - (FFN_CONV - 1) + k
                y = y + convw_ref[k:k + 1, cs] * ext_ref[off:off + tl, cs]
            halves.append(y)
        hidden = (_silu(halves[0]) * halves[1]).astype(BF16)
        acc = acc + jnp.dot(hidden, wdown_ref[ci * w:(ci + 1) * w, :], preferred_element_type=F32)
        us = nxt
    ext_ref[0:SUBLANES, :] = ext_ref[tl:tl + SUBLANES, :]
    out_ref[...] = _layer_norm(DEEPNORM_ALPHA * x + acc, lng_ref[...], lnb_ref[...])


def _ffn_call(layer, bsz, seqlen, x2, pp):
    tl = FFN_TILE
    nl = seqlen // tl
    row = lambda n: pl.BlockSpec((tl, n), lambda b, l: (b * nl + l, 0))
    return pl.pallas_call(
        _ffn_kernel,
        grid=(bsz, nl),
        in_specs=[row(D_MODEL),
                  _const_spec((D_MODEL, 2 * D_FF), layer), _const_spec((SUBLANES, 2 * D_FF), layer),
                  _const_spec((D_FF, D_MODEL), layer),
                  _const_spec((1, D_MODEL), layer), _const_spec((1, D_MODEL), layer)],
        out_specs=row(D_MODEL),
        out_shape=jax.ShapeDtypeStruct((bsz * seqlen, D_MODEL), F32),
        scratch_shapes=[pltpu.VMEM((tl + SUBLANES, 2 * D_FF), F32)],
        compiler_params=_params(("arbitrary", "arbitrary")),
        name="convffn",
    )(x2, pp["w_up"], pp["ffn_conv"], pp["w_down"], pp["ln2_g"], pp["ln2_b"])


def _pad_rows(a, rows):
    return jnp.pad(a, ((0, 0), (0, rows - a.shape[1]), (0, 0)))


def _s5_tables(a_re, a_im, log_dt, b_re, b_im, c_re, c_im, d):
    c = S5_CHUNK
    dt = jnp.exp(log_dt)[..., None]
    mag = jnp.exp(dt * a_re)
    abar_re, abar_im = mag * jnp.cos(dt * a_im), mag * jnp.sin(dt * a_im)
    den = a_re * a_re + a_im * a_im
    nr, ni = abar_re - 1.0, abar_im
    fr, fi = (nr * a_re + ni * a_im) / den, (ni * a_re - nr * a_im) / den
    bb_re = fr[..., None] * b_re - fi[..., None] * b_im
    bb_im = fr[..., None] * b_im + fi[..., None] * b_re
    j = jnp.arange(c + 1, dtype=F32)[:, None, None, None]
    pmag = jnp.exp(j * (dt * a_re)[None])
    pw_re = pmag * jnp.cos(j * (dt * a_im)[None])
    pw_im = pmag * jnp.sin(j * (dt * a_im)[None])
    cb_re = jnp.einsum('lgon,lgni->lgnoi', c_re, bb_re) - jnp.einsum('lgon,lgni->lgnoi', c_im, bb_im)
    cb_im = jnp.einsum('lgon,lgni->lgnoi', c_re, bb_im) + jnp.einsum('lgon,lgni->lgnoi', c_im, bb_re)
    kern = (jnp.einsum('jlgn,lgnoi->jlgoi', pw_re[:c], cb_re)
            - jnp.einsum('jlgn,lgnoi->jlgoi', pw_im[:c], cb_im))
    eye = jnp.eye(S5_GROUP, dtype=F32)
    kern = kern.at[0].add(d.reshape(d.shape[0], S5_GROUPS, S5_GROUP)[..., None] * eye)
    kcat = kern.transpose(1, 2, 4, 0, 3).reshape(d.shape[0], S5_GROUPS, S5_GROUP, c * S5_GROUP)
    rp_re, rp_im = pw_re[:c][::-1], pw_im[:c][::-1]
    p_re = rp_re[..., None] * bb_re[None] - rp_im[..., None] * bb_im[None]
    p_im = rp_re[..., None] * bb_im[None] + rp_im[..., None] * bb_re[None]
    tos = lambda a: a.transpose(1, 2, 3, 0, 4).reshape(d.shape[0], S5_GROUPS, S5_STATE, c * S5_GROUP)
    q_re = (jnp.einsum('lgon,tlgn->lgnto', c_re, pw_re[1:]) - jnp.einsum('lgon,tlgn->lgnto', c_im, pw_im[1:]))
    q_im = -(jnp.einsum('lgon,tlgn->lgnto', c_re, pw_im[1:]) + jnp.einsum('lgon,tlgn->lgnto', c_im, pw_re[1:]))
    toq = lambda a: a.reshape(d.shape[0], S5_GROUPS, S5_STATE, c * S5_GROUP)
    ac = jnp.stack([pw_re[c], pw_im[c]], axis=-1)
    ac = jnp.pad(ac, ((0, 0), (0, 0), (0, 0), (0, LANES - 2)))
    return dict(s5_kc=kcat, s5_ptr=tos(p_re).astype(BF16), s5_pti=tos(p_im).astype(BF16),
                s5_qr=toq(q_re).astype(BF16), s5_qi=toq(q_im).astype(BF16), s5_ac=ac)


def _pack_params(w_in, dn_conv, dn_a_log, dn_dt_bias, dn_norm, w_br_dn, cf_dw, cf_dw_bias, cf_ln_g,
                 cf_ln_b, w_br_cf, s5_a_re, s5_a_im, s5_log_dt, s5_b_re, s5_b_im, s5_c_re, s5_c_im,
                 s5_d, w_br_s5, gla_w_alpha, gla_b_alpha, gla_norm, w_br_gla, w_o, ln1_g, ln1_b,
                 w_up, ffn_conv, w_down, ln2_g, ln2_b):
    nl = w_in.shape[0]
    sizes = (DN_QKV, DN_HEADS, DN_HEADS, 512, 2 * CF_WIDTH, S5_WIDTH, GLA_QK, GLA_QK, GLA_VW, GLA_VW,
             GLA_RANK, N_BRANCH * D_MODEL)
    offs = [0]
    for s in sizes:
        offs.append(offs[-1] + s)
    col = lambda i: w_in[:, :, offs[i]:offs[i + 1]]
    w_small = jnp.concatenate([col(1), col(2), col(10)], axis=-1)
    w_small = jnp.pad(w_small, ((0, 0), (0, 0), (0, LANES - w_small.shape[-1])))
    ws_hi = w_small.astype(BF16)
    ws_lo = (w_small - ws_hi.astype(F32)).astype(BF16)
    pad_l = lambda a: jnp.pad(a, ((0, 0), (0, LANES - a.shape[-1])))
    dn_vec = jnp.stack([pad_l(-jnp.exp(dn_a_log)), pad_l(dn_dt_bias)], axis=1)
    dn_vec = _pad_rows(dn_vec, SUBLANES)
    wal = jnp.pad(gla_w_alpha, ((0, 0), (2 * DN_HEADS, LANES - 2 * DN_HEADS - GLA_RANK), (0, 0)))
    wal_hi = wal.astype(BF16)
    wal_lo = (wal - wal_hi.astype(F32)).astype(BF16)
    vec = lambda a: a[:, None, :]
    pp = dict(
        w_qkv=col(0).astype(BF16), w_mid=w_in[:, :, offs[3]:offs[10]].astype(BF16),
        w_gate=col(11).astype(BF16), ws_hi=ws_hi, ws_lo=ws_lo, dn_vec=dn_vec, wal_hi=wal_hi, wal_lo=wal_lo,
        b_alpha=vec(gla_b_alpha),
        dn_conv=dn_conv, dn_norm=vec(dn_norm), gla_norm=vec(gla_norm),
        w_br_dn=w_br_dn.astype(BF16), w_br_cf=w_br_cf.astype(BF16), w_br_s5=w_br_s5.astype(BF16),
        w_br_gla=w_br_gla.astype(BF16), w_o=w_o.astype(BF16),
        cf_dw=_pad_rows(cf_dw, CF_HALO), cf_dw_bias=vec(cf_dw_bias), cf_ln_g=vec(cf_ln_g),
        cf_ln_b=vec(cf_ln_b), ln1_g=vec(ln1_g), ln1_b=vec(ln1_b),
        w_up=w_up.astype(BF16), ffn_conv=_pad_rows(ffn_conv, SUBLANES), w_down=w_down.astype(BF16),
        ln2_g=vec(ln2_g), ln2_b=vec(ln2_b),
    )
    pp.update(_s5_tables(s5_a_re, s5_a_im, s5_log_dt, s5_b_re, s5_b_im, s5_c_re, s5_c_im, s5_d))
    del nl
    return pp


def kernel(x, w_in, dn_conv, dn_a_log, dn_dt_bias, dn_norm, w_br_dn, cf_dw, cf_dw_bias, cf_ln_g, cf_ln_b, w_br_cf, s5_a_re, s5_a_im, s5_log_dt, s5_b_re, s5_b_im, s5_c_re, s5_c_im, s5_d, w_br_s5, gla_w_alpha, gla_b_alpha, gla_norm, w_br_gla, w_o, ln1_g, ln1_b, w_up, ffn_conv, w_down, ln2_g, ln2_b):
    bsz, seqlen, d_model = x.shape
    assert d_model == D_MODEL and bsz == SUBLANES
    assert all(seqlen % t == 0 for t in (SEQ_TILE, MERGE_TILE, FFN_TILE, S5_CHUNK))
    assert (bsz * seqlen) % PROJ_TILE == 0
    depth = w_in.shape[0]
    pp = _pack_params(w_in, dn_conv, dn_a_log, dn_dt_bias, dn_norm, w_br_dn, cf_dw, cf_dw_bias,
                      cf_ln_g, cf_ln_b, w_br_cf, s5_a_re, s5_a_im, s5_log_dt, s5_b_re, s5_b_im,
                      s5_c_re, s5_c_im, s5_d, w_br_s5, gla_w_alpha, gla_b_alpha, gla_norm, w_br_gla,
                      w_o, ln1_g, ln1_b, w_up, ffn_conv, w_down, ln2_g, ln2_b)
    t = bsz * seqlen
    x2 = x.reshape(t, D_MODEL)
    for layer in range(depth):
        (qkv, dng, cf, s5_in, gqk, gv, gg, gates, small, loga) = _proj_call(layer, x2, pp)
        smallt = small[:, :SUBLANES].reshape(t // DN_CHUNK, DN_CHUNK, SUBLANES).transpose(0, 2, 1)
        a_in = _dn_call(layer, bsz, seqlen, qkv, small, smallt, dng, pp)
        d_in = _gla_call(layer, bsz, seqlen, gqk, gv, gg, loga, pp)
        z = _s5_call(layer, bsz, s5_in, pp)
        x2 = _merge_call(layer, bsz, seqlen, x2, a_in, cf, z, d_in, gates, pp)
        x2 = _ffn_call(layer, bsz, seqlen, x2, pp)
    return x2.reshape(bsz, seqlen, D_MODEL)
```

```python
import functools
import math

import jax
import jax.numpy as jnp
from jax import lax
from jax.experimental import pallas as pl
from jax.experimental.pallas import tpu as pltpu

F32 = jnp.float32
BF16 = jnp.bfloat16

D_MODEL = 1024
DN_HEADS = 4
DN_DK = 128
DN_CONV = 4
DN_CHUNK = 64
DN_QKV = 1536
CF_WIDTH = 512
CF_KERNEL = 31
S5_WIDTH = 512
S5_GROUP = 16
S5_GROUPS = 32
S5_STATE = 64
S5_CHUNK = 32
GLA_HEADS = 4
GLA_DK = 64
GLA_DV = 128
GLA_QK = 256
GLA_VW = 512
GLA_RANK = 16
GLA_TAU = 16.0
GLA_CHUNK = 64
GLA_SUB = 16
N_BRANCH = 4
D_FF = 2816
FFN_CONV = 3
FFN_COLS = 256
LN_EPS = 1e-5
DEPTH = 4
DEEPNORM_ALPHA = (2.0 * DEPTH) ** 0.25

LANES = 128
SUBLANES = 8
DN_TILE = 512
SEQ_TILE = 512
MERGE_TILE = 256
FFN_TILE = 256
PROJ_TILE = 512
VMEM_LIMIT = 56 * 1024 * 1024

C_DNG = 0
C_CF = 512
C_S5 = 1536
C_GQK = 2048
C_GV = 2560
C_GG = 3072
W_MID = 3584


def _dot(a, b):
    return jnp.dot(a.astype(BF16), b.astype(BF16), preferred_element_type=F32)


def _dot_nt(a, b):
    return lax.dot_general(a.astype(BF16), b.astype(BF16), (((1,), (1,)), ((), ())),
                           preferred_element_type=F32)


def _dot_tn(a, b):
    return lax.dot_general(a.astype(BF16), b.astype(BF16), (((0,), (0,)), ((), ())),
                           preferred_element_type=F32)


def _split3(x):
    x1 = x.astype(BF16)
    r = x - x1.astype(F32)
    x2 = r.astype(BF16)
    x3 = (r - x2.astype(F32)).astype(BF16)
    return x1, x2, x3


def _split2(x):
    x1 = x.astype(BF16)
    x2 = (x - x1.astype(F32)).astype(BF16)
    return x1, x2


def _dot01_left(m01, x):
    x1, x2, x3 = _split3(x)
    d = lambda v: jnp.dot(m01, v, preferred_element_type=F32)
    return d(x3) + d(x2) + d(x1)


def _dot01_right(x, m01):
    x1, x2, x3 = _split3(x)
    d = lambda v: jnp.dot(v, m01, preferred_element_type=F32)
    return d(x3) + d(x2) + d(x1)


def _dot_hilo(a, b_hi, b_lo):
    a_hi, a_lo = _split2(a)
    d = lambda u, v: jnp.dot(u, v, preferred_element_type=F32)
    return d(a_lo, b_hi) + d(a_hi, b_lo) + d(a_hi, b_hi)


def _sigmoid(x):
    return 1.0 / (1.0 + jnp.exp(-x))


def _silu(x):
    return x * _sigmoid(x)


def _softplus(x):
    return jnp.maximum(x, 0.0) + jnp.log(1.0 + jnp.exp(-jnp.abs(x)))


def _gelu_tanh(x):
    c = math.sqrt(2.0 / math.pi)
    return 0.5 * x * (1.0 + jnp.tanh(c * (x + 0.044715 * (x * x * x))))


def _layer_norm(x, g, b):
    mu = jnp.mean(x, axis=-1, keepdims=True)
    xc = x - mu
    var = jnp.mean(xc * xc, axis=-1, keepdims=True)
    return xc * lax.rsqrt(var + LN_EPS) * g + b


def _tri(n, kind):
    r = lax.broadcasted_iota(jnp.int32, (n, n), 0)
    c = lax.broadcasted_iota(jnp.int32, (n, n), 1)
    if kind == "lower":
        return r >= c
    if kind == "strict":
        return r > c
    if kind == "upper":
        return r <= c
    raise ValueError(kind)


def _const_spec(shape, layer):
    nd = len(shape)
    return pl.BlockSpec((None,) + tuple(shape), lambda *_: (layer,) + (0,) * nd,
                        pipeline_mode=pl.Buffered(1))


def _params(sem):
    return pltpu.CompilerParams(dimension_semantics=sem, vmem_limit_bytes=VMEM_LIMIT)


def _proj_kernel(x_ref, wq_ref, w_ref, wg_ref, wsh_ref, wsl_ref, vec_ref, walh_ref, wall_ref, bal_ref,
                 qkv_ref, dng_ref, cf_ref, s5_ref, gqk_ref, gv_ref, gg_ref, gates_ref,
                 small_ref, loga_ref):
    x = x_ref[...]
    xb = x.astype(BF16)

    def mm(c0, n):
        return jnp.dot(xb, w_ref[:, c0:c0 + n], preferred_element_type=F32)

    for j in range(DN_QKV // 512):
        qkv_ref[:, j * 512:(j + 1) * 512] = jnp.dot(
            xb, wq_ref[:, j * 512:(j + 1) * 512], preferred_element_type=F32).astype(BF16)
    dng_ref[...] = _silu(mm(C_DNG, 512)).astype(BF16)
    cf_a = mm(C_CF, 512)
    cf_g = mm(C_CF + 512, 512)
    cf_ref[...] = (cf_a * _sigmoid(cf_g)).astype(BF16)
    s5_ref[...] = mm(C_S5, 512).astype(BF16)
    gqk_ref[:, 0:GLA_QK] = (mm(C_GQK, GLA_QK) * (GLA_DK ** -0.5)).astype(BF16)
    gqk_ref[:, GLA_QK:2 * GLA_QK] = mm(C_GQK + GLA_QK, GLA_QK).astype(BF16)
    gv_ref[...] = mm(C_GV, 512).astype(BF16)
    gg_ref[...] = _silu(mm(C_GG, 512)).astype(BF16)
    for j in range(N_BRANCH * D_MODEL // 512):
        gates_ref[:, j * 512:(j + 1) * 512] = _sigmoid(jnp.dot(
            xb, wg_ref[:, j * 512:(j + 1) * 512], preferred_element_type=F32)).astype(BF16)

    x_lo = (x - xb.astype(F32)).astype(BF16)
    d = lambda u, v: jnp.dot(u, v, preferred_element_type=F32)
    s = d(x_lo, wsh_ref[...]) + d(xb, wsl_ref[...]) + d(xb, wsh_ref[...])
    lane = lax.broadcasted_iota(jnp.int32, s.shape, 1)
    neg_a = vec_ref[0:1, :]
    dt_b = vec_ref[1:2, :]
    g = neg_a * _softplus(s + dt_b)
    beta = _sigmoid(s)
    small = jnp.where(lane < DN_HEADS, g, jnp.where(lane < 2 * DN_HEADS, beta, s))
    small_ref[...] = small
    z = _dot_hilo(s, walh_ref[...], wall_ref[...]) + bal_ref[...]
    log_sig = jnp.minimum(z, 0.0) - jnp.log(1.0 + jnp.exp(-jnp.abs(z)))
    loga_ref[...] = log_sig * (1.0 / GLA_TAU)


def _proj_call(layer, x2, pp):
    t = x2.shape[0]
    tm = PROJ_TILE
    row = lambda n: pl.BlockSpec((tm, n), lambda i: (i, 0))
    outs = [(DN_QKV, BF16), (512, BF16), (512, BF16), (512, BF16), (512, BF16), (512, BF16),
            (512, BF16), (N_BRANCH * D_MODEL, BF16), (LANES, F32), (GLA_QK, F32)]
    return pl.pallas_call(
        _proj_kernel,
        grid=(t // tm,),
        in_specs=[row(D_MODEL),
                  _const_spec((D_MODEL, DN_QKV), layer),
                  _const_spec((D_MODEL, W_MID), layer),
                  _const_spec((D_MODEL, N_BRANCH * D_MODEL), layer),
                  _const_spec((D_MODEL, LANES), layer),
                  _const_spec((D_MODEL, LANES), layer),
                  _const_spec((SUBLANES, LANES), layer),
                  _const_spec((LANES, GLA_QK), layer),
                  _const_spec((LANES, GLA_QK), layer),
                  _const_spec((1, GLA_QK), layer)],
        out_specs=[row(n) for n, _ in outs],
        out_shape=[jax.ShapeDtypeStruct((t, n), dt) for n, dt in outs],
        compiler_params=_params(("arbitrary",)),
        name="proj",
    )(x2, pp["w_qkv"], pp["w_mid"], pp["w_gate"], pp["ws_hi"], pp["ws_lo"], pp["dn_vec"], pp["wal_hi"], pp["wal_lo"],
      pp["b_alpha"])


def _dn_kernel(qkv_ref, small_ref, smallt_ref, gate_ref, convw_ref, norm_ref, out_ref,
               ext_ref, qkvc_ref, st_ref):
    tl = qkv_ref.shape[0]
    c = DN_CHUNK

    @pl.when(pl.program_id(1) == 0)
    def _():
        ext_ref[0:SUBLANES, :] = jnp.zeros((SUBLANES, DN_QKV), F32)
        st_ref[...] = jnp.zeros(st_ref.shape, F32)

    ext_ref[SUBLANES:SUBLANES + tl, :] = qkv_ref[...].astype(F32)

    causal = _tri(c, "lower")
    strict = _tri(c, "strict")
    tril01 = causal.astype(BF16)
    triu01 = _tri(c, "upper").astype(BF16)
    eye = (lax.broadcasted_iota(jnp.int32, (c, c), 0)
           == lax.broadcasted_iota(jnp.int32, (c, c), 1)).astype(F32)
    norm = norm_ref[...]

    def conv(r0, nrows):
        for j in range(DN_QKV // LANES):
            sl = slice(j * LANES, (j + 1) * LANES)
            acc = convw_ref[DN_CONV - 1:DN_CONV, sl] * ext_ref[SUBLANES + r0:SUBLANES + r0 + nrows, sl]
            for k in range(DN_CONV - 1):
                off = SUBLANES - (DN_CONV - 1) + k + r0
                acc = acc + convw_ref[k:k + 1, sl] * ext_ref[off:off + nrows, sl]
            y = _silu(acc)
            if j < 2 * DN_HEADS:
                y = y * lax.rsqrt(jnp.sum(y * y, axis=-1, keepdims=True) + 1e-6)
                if j < DN_HEADS:
                    y = y * (DN_DK ** -0.5)
            qkvc_ref[r0:r0 + nrows, sl] = y

    def prep_vector(chunks):
        work = []
        for ci in chunks:
            r0 = ci * c
            sm = small_ref[r0:r0 + c, :]
            gc_all = _dot01_left(tril01, sm)
            gcr_all = _dot01_right(smallt_ref[ci], triu01)
            exp_gc = jnp.exp(gc_all)
            g_last = gc_all[c - 1:c, :]
            exp_rem = jnp.exp(g_last - gc_all)
            exp_last = jnp.exp(g_last)
            for h in range(DN_HEADS):
                q = qkvc_ref[r0:r0 + c, h * LANES:(h + 1) * LANES]
                k = qkvc_ref[r0:r0 + c, (4 + h) * LANES:(5 + h) * LANES]
                v = qkvc_ref[r0:r0 + c, (8 + h) * LANES:(9 + h) * LANES]
                beta = sm[:, DN_HEADS + h:DN_HEADS + h + 1]
                diff = gc_all[:, h:h + 1] - gcr_all[h:h + 1, :]
                kb = k * beta
                work.append(dict(
                    r0=r0, h=h, q=q.astype(BF16), k=k.astype(BF16), kb=kb.astype(BF16),
                    decay=jnp.where(causal, jnp.exp(jnp.where(causal, diff, 0.0)), 0.0),
                    rhs=jnp.concatenate([v * beta, kb * exp_gc[:, h:h + 1]], axis=1).astype(BF16),
                    qd=q * exp_gc[:, h:h + 1], kd=(k * exp_rem[:, h:h + 1]).astype(BF16),
                    last=exp_last[:, h:h + 1]))
        return work

    def solve(work):
        ms = [-jnp.where(strict, _dot_nt(w["kb"], w["k"]) * w["decay"], 0.0) for w in work]
        amat = [(_dot_nt(w["q"], w["k"]) * w["decay"]).astype(BF16) for w in work]
        ps = [eye + m for m in ms]
        for _ in range(int(math.log2(c)) - 1):
            ms = [_dot(m, m) for m in ms]
            ps = [p + _dot(p, m) for p, m in zip(ps, ms)]
        sols = [_dot(p, w["rhs"]) for p, w in zip(ps, work)]
        us = [s[:, :LANES].astype(BF16) for s in sols]
        ws = [s[:, LANES:].astype(BF16) for s in sols]
        for w, a, u, ww in zip(work, amat, us, ws):
            w["qeff"] = w["qd"] - _dot(a, ww)
            w["oc"] = _dot(a, u)
            w["gmat"] = _dot_tn(w["kd"], ww)
            w["bmat"] = _dot_tn(w["kd"], u)

    def apply_state(work):
        for w in work:
            r0, h = w["r0"], w["h"]
            hs = slice(h * LANES, (h + 1) * LANES)
            s = st_ref[h]
            sb = s.astype(BF16)
            o = _dot(w["qeff"], sb) + w["oc"]
            st_ref[h] = s * w["last"] - _dot(w["gmat"], sb) + w["bmat"]
            o = o * lax.rsqrt(jnp.mean(o * o, axis=-1, keepdims=True) + LN_EPS) * norm
            out_ref[r0:r0 + c, hs] = (o * gate_ref[r0:r0 + c, hs].astype(F32)).astype(BF16)

    conv(0, tl)
    ext_ref[0:SUBLANES, :] = ext_ref[tl:tl + SUBLANES, :]
    work = prep_vector(range(tl // c))
    solve(work)
    apply_state(work)


def _dn_call(layer, bsz, seqlen, qkv, small, smallt, gate, pp):
    tl = DN_TILE
    nl = seqlen // tl
    row = lambda n: pl.BlockSpec((tl, n), lambda b, l: (b * nl + l, 0))
    return pl.pallas_call(
        _dn_kernel,
        grid=(bsz, nl),
        in_specs=[row(DN_QKV), row(LANES),
                  pl.BlockSpec((tl // DN_CHUNK, SUBLANES, DN_CHUNK), lambda b, l: (b * nl + l, 0, 0)),
                  row(512),
                  _const_spec((DN_CONV, DN_QKV), layer),
                  _const_spec((1, LANES), layer)],
        out_specs=row(512),
        out_shape=jax.ShapeDtypeStruct((bsz * seqlen, 512), BF16),
        scratch_shapes=[pltpu.VMEM((tl + SUBLANES, DN_QKV), F32),
                        pltpu.VMEM((tl, DN_QKV), F32),
                        pltpu.VMEM((DN_HEADS, DN_DK, LANES), F32)],
        compiler_params=_params(("arbitrary", "arbitrary")),
        name="deltanet",
    )(qkv, small, smallt, gate, pp["dn_conv"], pp["dn_norm"])


def _gla_kernel(qk_ref, v_ref, g_ref, loga_ref, norm_ref, out_ref, st_ref):
    tl = qk_ref.shape[0]
    c = GLA_CHUNK
    nsub = c // GLA_SUB

    @pl.when(pl.program_id(1) == 0)
    def _():
        st_ref[...] = jnp.zeros(st_ref.shape, F32)

    tril01 = _tri(c, "lower").astype(BF16)
    rows = lax.broadcasted_iota(jnp.int32, (c, LANES), 0)
    lane = lax.broadcasted_iota(jnp.int32, (GLA_SUB, LANES), 1)
    srow = lax.broadcasted_iota(jnp.int32, (GLA_SUB, c), 0)
    scol = lax.broadcasted_iota(jnp.int32, (GLA_SUB, c), 1)
    st_r = lax.broadcasted_iota(jnp.int32, (2 * GLA_DV, LANES), 0) >= GLA_DV
    st_c = lax.broadcasted_iota(jnp.int32, (2 * GLA_DV, LANES), 1) >= GLA_DK
    st_mask = st_r == st_c
    norm = norm_ref[...]

    work = []
    for ci in range(tl // c):
        r0 = ci * c
        gcum = _dot01_left(tril01, loga_ref[r0:r0 + c, :])
        q = qk_ref[r0:r0 + c, 0:GLA_QK].astype(F32)
        k = qk_ref[r0:r0 + c, GLA_QK:2 * GLA_QK].astype(F32)
        g_end = gcum[c - 1:c, :]
        refs = [jnp.zeros((1, GLA_QK), F32)] + [gcum[GLA_SUB * i - 1:GLA_SUB * i, :] for i in range(1, nsub)]
        ref_rows = jnp.concatenate([jnp.broadcast_to(r, (GLA_SUB, GLA_QK)) for r in refs], axis=0)
        qn = q * jnp.exp(gcum - ref_rows)
        qdec = q * jnp.exp(gcum)
        kdec = k * jnp.exp(g_end - gcum)
        for p in range(GLA_HEADS // 2):
            ps = slice(p * LANES, (p + 1) * LANES)
            kp = k[:, ps]
            gp = gcum[:, ps]
            sc = [[], []]
            for i in range(nsub):
                e = jnp.where(rows < GLA_SUB * (i + 1), refs[i][:, ps] - gp, 0.0)
                kn = (kp * jnp.exp(e)).astype(BF16)
                qi = qn[GLA_SUB * i:GLA_SUB * (i + 1), ps]
                for hh in range(2):
                    lhs = jnp.where((lane >= GLA_DK) if hh else (lane < GLA_DK), qi, 0.0)
                    s = _dot_nt(lhs, kn)
                    sc[hh].append(jnp.where(scol <= srow + GLA_SUB * i, s, 0.0))
            vp = v_ref[r0:r0 + c, p * 2 * GLA_DV:(p + 1) * 2 * GLA_DV]
            upd = jnp.where(st_mask, _dot_tn(vp, kdec[:, ps]), 0.0)
            intra = [_dot(jnp.concatenate(sc[hh], axis=0),
                          v_ref[r0:r0 + c, (2 * p + hh) * GLA_DV:(2 * p + hh + 1) * GLA_DV])
                     for hh in range(2)]
            work.append((r0, p, qdec[:, ps].astype(BF16), upd, jnp.exp(g_end[:, ps]), intra))

    for r0, p, qd, upd, dec, intra in work:
        st = st_ref[p]
        o_inter = _dot_nt(qd, st)
        st_ref[p] = st * dec + upd
        for hh in range(2):
            hs = slice((2 * p + hh) * GLA_DV, (2 * p + hh + 1) * GLA_DV)
            o = intra[hh] + o_inter[:, hh * GLA_DV:(hh + 1) * GLA_DV]
            o = o * lax.rsqrt(jnp.mean(o * o, axis=-1, keepdims=True) + LN_EPS) * norm
            out_ref[r0:r0 + c, hs] = (o * g_ref[r0:r0 + c, hs].astype(F32)).astype(BF16)


def _gla_call(layer, bsz, seqlen, gqk, gv, gg, loga, pp):
    tl = SEQ_TILE
    nl = seqlen // tl
    row = lambda n: pl.BlockSpec((tl, n), lambda b, l: (b * nl + l, 0))
    return pl.pallas_call(
        _gla_kernel,
        grid=(bsz, nl),
        in_specs=[row(2 * GLA_QK), row(GLA_VW), row(GLA_VW), row(GLA_QK),
                  _const_spec((1, LANES), layer)],
        out_specs=row(GLA_VW),
        out_shape=jax.ShapeDtypeStruct((bsz * seqlen, GLA_VW), BF16),
        scratch_shapes=[pltpu.VMEM((GLA_HEADS // 2, 2 * GLA_DV, 2 * GLA_DK), F32)],
        compiler_params=_params(("arbitrary", "arbitrary")),
        name="gla",
    )(gqk, gv, gg, loga, pp["gla_norm"])


S5_GPT = LANES // S5_GROUP


def _s5_kernel(u_ref, kc_ref, ptr_ref, pti_ref, qr_ref, qi_ref, ac_ref, y_ref,
               xs_ref, ug_ref, yg_ref, ys_ref, t_ref):
    c = S5_CHUNK
    nch = u_ref.shape[0] // c
    xs_ref[...] = u_ref[...].astype(F32)
    lane = lax.broadcasted_iota(jnp.int32, (nch, LANES), 1)
    piece = [(lane >= S5_GROUP * i) & (lane < S5_GROUP * (i + 1)) for i in range(S5_GPT)]

    for q in range(c // S5_GPT):
        tiles = [None] * S5_GPT
        for i in range(S5_GPT):
            r = xs_ref[pl.ds(S5_GPT * q + i, nch, stride=c), :]
            for g in range(S5_GPT):
                sh = (S5_GROUP * (i - g)) % LANES
                rr = pltpu.roll(r, sh, axis=1) if sh else r
                tiles[g] = jnp.where(piece[i], rr, 0.0 if tiles[g] is None else tiles[g])
        for g in range(S5_GPT):
            ug_ref[g, :, LANES * q:LANES * (q + 1)] = tiles[g].astype(BF16)

    klane = lax.broadcasted_iota(jnp.int32, (S5_GROUP, c * S5_GROUP), 1)
    clane = lax.broadcasted_iota(jnp.int32, (S5_STATE, nch), 1)
    xrs, xis = [], []
    for g in range(S5_GPT):
        kc = kc_ref[g]
        for s in range(c):
            blk = kc if s == 0 else jnp.where(klane >= S5_GROUP * s,
                                              pltpu.roll(kc, S5_GROUP * s, axis=1), 0.0)
            t_ref[g, S5_GROUP * s:S5_GROUP * (s + 1), :] = blk.astype(BF16)
        u = ug_ref[g]
        yg_ref[g] = jnp.dot(u, t_ref[g], preferred_element_type=F32)
        xrs.append(_dot_nt(ptr_ref[g], u))
        xis.append(_dot_nt(pti_ref[g], u))
    prs = [ac_ref[g][:, 0:1] for g in range(S5_GPT)]
    pis = [ac_ref[g][:, 1:2] for g in range(S5_GPT)]
    dist = 1
    while dist < nch:
        for g in range(S5_GPT):
            sr = jnp.where(clane >= dist, pltpu.roll(xrs[g], dist, axis=1), 0.0)
            si = jnp.where(clane >= dist, pltpu.roll(xis[g], dist, axis=1), 0.0)
            pr, pi = prs[g], pis[g]
            xrs[g], xis[g] = xrs[g] + pr * sr - pi * si, xis[g] + pr * si + pi * sr
            prs[g], pis[g] = pr * pr - pi * pi, 2.0 * pr * pi
        dist *= 2
    for g in range(S5_GPT):
        xr = jnp.where(clane >= 1, pltpu.roll(xrs[g], 1, axis=1), 0.0)
        xi = jnp.where(clane >= 1, pltpu.roll(xis[g], 1, axis=1), 0.0)
        yg_ref[g] = yg_ref[g] + _dot_tn(xr, qr_ref[g]) + _dot_tn(xi, qi_ref[g])

    for s in range(c):
        q, i = divmod(s, S5_GPT)
        tile = None
        for g in range(S5_GPT):
            r = yg_ref[g, :, LANES * q:LANES * (q + 1)]
            sh = (S5_GROUP * (g - i)) % LANES
            if sh:
                r = pltpu.roll(r, sh, axis=1)
            tile = jnp.where(piece[g], r, 0.0 if tile is None else tile)
        ys_ref[pl.ds(s, nch, stride=c), :] = tile
    y_ref[...] = ys_ref[...].astype(BF16)


def _s5_call(layer, bsz, u2, pp):
    c = S5_CHUNK
    seqlen = u2.shape[0] // bsz
    nch = seqlen // c
    assert nch <= LANES, "the chunk scan keeps one sequence's chunks inside one lane tile"
    tab = lambda a, b: pl.BlockSpec((None, S5_GPT, a, b), lambda bi, j: (layer, j, 0, 0))
    io = pl.BlockSpec((seqlen, LANES), lambda bi, j: (bi, j))
    return pl.pallas_call(
        _s5_kernel,
        grid=(bsz, u2.shape[1] // LANES),
        in_specs=[io, tab(S5_GROUP, c * S5_GROUP), tab(S5_STATE, c * S5_GROUP),
                  tab(S5_STATE, c * S5_GROUP), tab(S5_STATE, c * S5_GROUP),
                  tab(S5_STATE, c * S5_GROUP), tab(S5_STATE, LANES)],
        out_specs=io,
        out_shape=jax.ShapeDtypeStruct(u2.shape, BF16),
        scratch_shapes=[pltpu.VMEM((seqlen, LANES), F32),
                        pltpu.VMEM((S5_GPT, nch, c * S5_GROUP), BF16),
                        pltpu.VMEM((S5_GPT, nch, c * S5_GROUP), F32),
                        pltpu.VMEM((seqlen, LANES), F32),
                        pltpu.VMEM((S5_GPT, c * S5_GROUP, c * S5_GROUP), BF16)],
        compiler_params=_params(("arbitrary", "arbitrary")),
        name="s5",
    )(u2, pp["s5_kc"], pp["s5_ptr"], pp["s5_pti"], pp["s5_qr"], pp["s5_qi"], pp["s5_ac"])


CF_HALO = 32


def _merge_kernel(x_ref, a_ref, cf_ref, z_ref, d_ref, gates_ref,
                  wdn_ref, wcf_ref, ws5_ref, wgla_ref, wo_ref,
                  cfw_ref, cfb_ref, cfg_ref, cfbeta_ref, lng_ref, lnb_ref,
                  out_ref, rot_ref, conv_ref):
    tl = x_ref.shape[0]

    @pl.when(pl.program_id(1) == 0)
    def _():
        rot_ref[0, 0:CF_HALO, :] = jnp.zeros((CF_HALO, CF_WIDTH), F32)

    y_a = jnp.dot(a_ref[...], wdn_ref[...], preferred_element_type=F32)
    y_d = jnp.dot(d_ref[...], wgla_ref[...], preferred_element_type=F32)
    zg = _gelu_tanh(z_ref[...].astype(F32)).astype(BF16)
    z_val = jnp.dot(zg, ws5_ref[:, 0:D_MODEL], preferred_element_type=F32)
    z_gate = jnp.dot(zg, ws5_ref[:, D_MODEL:2 * D_MODEL], preferred_element_type=F32)

    rot_ref[0, CF_HALO:CF_HALO + tl, :] = cf_ref[...].astype(F32)
    nrows = tl + CF_HALO - SUBLANES
    for r in range(1, SUBLANES):
        rot_ref[r, 0:nrows, :] = rot_ref[0, r:r + nrows, :]
    base = CF_HALO - (CF_KERNEL - 1)
    for j in range(CF_WIDTH // LANES):
        sl = slice(j * LANES, (j + 1) * LANES)
        acc = None
        for k in range(CF_KERNEL):
            r = (base + k) % SUBLANES
            q = base + k - r
            term = cfw_ref[k:k + 1, sl] * rot_ref[r, q:q + tl, sl]
            acc = term if acc is None else acc + term
        conv_ref[:, sl] = acc
    rot_ref[0, 0:CF_HALO, :] = rot_ref[0, tl:tl + CF_HALO, :]

    cfo = _silu(_layer_norm(conv_ref[...] + cfb_ref[...], cfg_ref[...], cfbeta_ref[...]))
    y_b = jnp.dot(cfo.astype(BF16), wcf_ref[...], preferred_element_type=F32)
    y_c = z_val * _sigmoid(z_gate)
    gt = lambda i: gates_ref[:, i * D_MODEL:(i + 1) * D_MODEL].astype(F32)
    merged = gt(0) * y_a + gt(1) * y_b + gt(2) * y_c + gt(3) * y_d
    mix = jnp.dot(merged.astype(BF16), wo_ref[...], preferred_element_type=F32)
    out_ref[...] = _layer_norm(DEEPNORM_ALPHA * x_ref[...] + mix, lng_ref[...], lnb_ref[...])


def _merge_call(layer, bsz, seqlen, x2, a_in, cf, z, d_in, gates, pp):
    tl = MERGE_TILE
    nl = seqlen // tl
    row = lambda n: pl.BlockSpec((tl, n), lambda b, l: (b * nl + l, 0))
    return pl.pallas_call(
        _merge_kernel,
        grid=(bsz, nl),
        in_specs=[row(D_MODEL), row(512), row(CF_WIDTH), row(S5_WIDTH), row(GLA_VW),
                  row(N_BRANCH * D_MODEL),
                  _const_spec((512, D_MODEL), layer), _const_spec((CF_WIDTH, D_MODEL), layer),
                  _const_spec((S5_WIDTH, 2 * D_MODEL), layer), _const_spec((GLA_VW, D_MODEL), layer),
                  _const_spec((D_MODEL, D_MODEL), layer),
                  _const_spec((CF_HALO, CF_WIDTH), layer), _const_spec((1, CF_WIDTH), layer),
                  _const_spec((1, CF_WIDTH), layer), _const_spec((1, CF_WIDTH), layer),
                  _const_spec((1, D_MODEL), layer), _const_spec((1, D_MODEL), layer)],
        out_specs=row(D_MODEL),
        out_shape=jax.ShapeDtypeStruct((bsz * seqlen, D_MODEL), F32),
        scratch_shapes=[pltpu.VMEM((SUBLANES, tl + CF_HALO, CF_WIDTH), F32),
                        pltpu.VMEM((tl, CF_WIDTH), F32)],
        compiler_params=_params(("arbitrary", "arbitrary")),
        name="merge",
    )(x2, a_in, cf, z, d_in, gates, pp["w_br_dn"], pp["w_br_cf"], pp["w_br_s5"], pp["w_br_gla"],
      pp["w_o"], pp["cf_dw"], pp["cf_dw_bias"], pp["cf_ln_g"], pp["cf_ln_b"], pp["ln1_g"], pp["ln1_b"])


def _ffn_kernel(x_ref, wup_ref, convw_ref, wdown_ref, lng_ref, lnb_ref, out_ref, ext_ref, hid_ref):
    tl = x_ref.shape[0]
    w = FFN_COLS

    @pl.when(pl.program_id(1) == 0)
    def _():
        ext_ref[0:SUBLANES, :] = jnp.zeros((SUBLANES, 2 * D_FF), F32)

    x = x_ref[...]
    xb = x.astype(BF16)
    nchunks = D_FF // w
    cols = lambda ci, half: slice(half * D_FF + ci * w, half * D_FF + (ci + 1) * w)

    def up(ci):
        us = []
        for half in range(2):
            u = jnp.dot(xb, wup_ref[:, cols(ci, half)], preferred_element_type=F32)
            ext_ref[SUBLANES:SUBLANES + tl, cols(ci, half)] = u
            us.append(u)
        return us

    us = up(0)
    for ci in range(nchunks):
        nxt = up(ci + 1) if ci + 1 < nchunks else None
        halves = []
        for half in range(2):
            cs = cols(ci, half)
            y = convw_ref[FFN_CONV - 1:FFN_CONV, cs] * us[half]
            for k in range(FFN_CONV - 1):
                off = SUBLANES - (FFN_CONV - 1) + k
                y = y + convw_ref[k:k + 1, cs] * ext_ref[off:off + tl, cs]
            halves.append(y)
        hid_ref[:, ci * w:(ci + 1) * w] = (_silu(halves[0]) * halves[1]).astype(BF16)
        us = nxt
    acc = jnp.dot(hid_ref[...], wdown_ref[...], preferred_element_type=F32)
    ext_ref[0:SUBLANES, :] = ext_ref[tl:tl + SUBLANES, :]
    out_ref[...] = _layer_norm(DEEPNORM_ALPHA * x + acc, lng_ref[...], lnb_ref[...])


def _ffn_call(layer, bsz, seqlen, x2, pp):
    tl = FFN_TILE
    nl = seqlen // tl
    row = lambda n: pl.BlockSpec((tl, n), lambda b, l: (b * nl + l, 0))
    return pl.pallas_call(
        _ffn_kernel,
        grid=(bsz, nl),
        in_specs=[row(D_MODEL),
                  _const_spec((D_MODEL, 2 * D_FF), layer), _const_spec((SUBLANES, 2 * D_FF), layer),
                  _const_spec((D_FF, D_MODEL), layer),
                  _const_spec((1, D_MODEL), layer), _const_spec((1, D_MODEL), layer)],
        out_specs=row(D_MODEL),
        out_shape=jax.ShapeDtypeStruct((bsz * seqlen, D_MODEL), F32),
        scratch_shapes=[pltpu.VMEM((tl + SUBLANES, 2 * D_FF), F32),
                        pltpu.VMEM((tl, D_FF), BF16)],
        compiler_params=_params(("arbitrary", "arbitrary")),
        name="convffn",
    )(x2, pp["w_up"], pp["ffn_conv"], pp["w_down"], pp["ln2_g"], pp["ln2_b"])


def _pad_rows(a, rows):
    return jnp.pad(a, ((0, 0), (0, rows - a.shape[1]), (0, 0)))


def _s5_tables(a_re, a_im, log_dt, b_re, b_im, c_re, c_im, d):
    c = S5_CHUNK
    dt = jnp.exp(log_dt)[..., None]
    mag = jnp.exp(dt * a_re)
    abar_re, abar_im = mag * jnp.cos(dt * a_im), mag * jnp.sin(dt * a_im)
    den = a_re * a_re + a_im * a_im
    nr, ni = abar_re - 1.0, abar_im
    fr, fi = (nr * a_re + ni * a_im) / den, (ni * a_re - nr * a_im) / den
    bb_re = fr[..., None] * b_re - fi[..., None] * b_im
    bb_im = fr[..., None] * b_im + fi[..., None] * b_re
    j = jnp.arange(c + 1, dtype=F32)[:, None, None, None]
    pmag = jnp.exp(j * (dt * a_re)[None])
    pw_re = pmag * jnp.cos(j * (dt * a_im)[None])
    pw_im = pmag * jnp.sin(j * (dt * a_im)[None])
    cb_re = jnp.einsum('lgon,lgni->lgnoi', c_re, bb_re) - jnp.einsum('lgon,lgni->lgnoi', c_im, bb_im)
    cb_im = jnp.einsum('lgon,lgni->lgnoi', c_re, bb_im) + jnp.einsum('lgon,lgni->lgnoi', c_im, bb_re)
    kern = (jnp.einsum('jlgn,lgnoi->jlgoi', pw_re[:c], cb_re)
            - jnp.einsum('jlgn,lgnoi->jlgoi', pw_im[:c], cb_im))
    eye = jnp.eye(S5_GROUP, dtype=F32)
    kern = kern.at[0].add(d.reshape(d.shape[0], S5_GROUPS, S5_GROUP)[..., None] * eye)
    kcat = kern.transpose(1, 2, 4, 0, 3).reshape(d.shape[0], S5_GROUPS, S5_GROUP, c * S5_GROUP)
    rp_re, rp_im = pw_re[:c][::-1], pw_im[:c][::-1]
    p_re = rp_re[..., None] * bb_re[None] - rp_im[..., None] * bb_im[None]
    p_im = rp_re[..., None] * bb_im[None] + rp_im[..., None] * bb_re[None]
    tos = lambda a: a.transpose(1, 2, 3, 0, 4).reshape(d.shape[0], S5_GROUPS, S5_STATE, c * S5_GROUP)
    q_re = (jnp.einsum('lgon,tlgn->lgnto', c_re, pw_re[1:]) - jnp.einsum('lgon,tlgn->lgnto', c_im, pw_im[1:]))
    q_im = -(jnp.einsum('lgon,tlgn->lgnto', c_re, pw_im[1:]) + jnp.einsum('lgon,tlgn->lgnto', c_im, pw_re[1:]))
    toq = lambda a: a.reshape(d.shape[0], S5_GROUPS, S5_STATE, c * S5_GROUP)
    ac = jnp.stack([pw_re[c], pw_im[c]], axis=-1)
    ac = jnp.pad(ac, ((0, 0), (0, 0), (0, 0), (0, LANES - 2)))
    return dict(s5_kc=kcat, s5_ptr=tos(p_re).astype(BF16), s5_pti=tos(p_im).astype(BF16),
                s5_qr=toq(q_re).astype(BF16), s5_qi=toq(q_im).astype(BF16), s5_ac=ac)


def _pack_params(w_in, dn_conv, dn_a_log, dn_dt_bias, dn_norm, w_br_dn, cf_dw, cf_dw_bias, cf_ln_g,
                 cf_ln_b, w_br_cf, s5_a_re, s5_a_im, s5_log_dt, s5_b_re, s5_b_im, s5_c_re, s5_c_im,
                 s5_d, w_br_s5, gla_w_alpha, gla_b_alpha, gla_norm, w_br_gla, w_o, ln1_g, ln1_b,
                 w_up, ffn_conv, w_down, ln2_g, ln2_b):
    nl = w_in.shape[0]
    sizes = (DN_QKV, DN_HEADS, DN_HEADS, 512, 2 * CF_WIDTH, S5_WIDTH, GLA_QK, GLA_QK, GLA_VW, GLA_VW,
             GLA_RANK, N_BRANCH * D_MODEL)
    offs = [0]
    for s in sizes:
        offs.append(offs[-1] + s)
    col = lambda i: w_in[:, :, offs[i]:offs[i + 1]]
    w_small = jnp.concatenate([col(1), col(2), col(10)], axis=-1)
    w_small = jnp.pad(w_small, ((0, 0), (0, 0), (0, LANES - w_small.shape[-1])))
    ws_hi = w_small.astype(BF16)
    ws_lo = (w_small - ws_hi.astype(F32)).astype(BF16)
    pad_l = lambda a: jnp.pad(a, ((0, 0), (0, LANES - a.shape[-1])))
    dn_vec = jnp.stack([pad_l(-jnp.exp(dn_a_log)), pad_l(dn_dt_bias)], axis=1)
    dn_vec = _pad_rows(dn_vec, SUBLANES)
    wal = jnp.pad(gla_w_alpha, ((0, 0), (2 * DN_HEADS, LANES - 2 * DN_HEADS - GLA_RANK), (0, 0)))
    wal_hi = wal.astype(BF16)
    wal_lo = (wal - wal_hi.astype(F32)).astype(BF16)
    vec = lambda a: a[:, None, :]
    pp = dict(
        w_qkv=col(0).astype(BF16), w_mid=w_in[:, :, offs[3]:offs[10]].astype(BF16),
        w_gate=col(11).astype(BF16), ws_hi=ws_hi, ws_lo=ws_lo, dn_vec=dn_vec, wal_hi=wal_hi, wal_lo=wal_lo,
        b_alpha=vec(gla_b_alpha),
        dn_conv=dn_conv, dn_norm=vec(dn_norm), gla_norm=vec(gla_norm),
        w_br_dn=w_br_dn.astype(BF16), w_br_cf=w_br_cf.astype(BF16), w_br_s5=w_br_s5.astype(BF16),
        w_br_gla=w_br_gla.astype(BF16), w_o=w_o.astype(BF16),
        cf_dw=_pad_rows(cf_dw, CF_HALO), cf_dw_bias=vec(cf_dw_bias), cf_ln_g=vec(cf_ln_g),
        cf_ln_b=vec(cf_ln_b), ln1_g=vec(ln1_g), ln1_b=vec(ln1_b),
        w_up=w_up.astype(BF16), ffn_conv=_pad_rows(ffn_conv, SUBLANES), w_down=w_down.astype(BF16),
        ln2_g=vec(ln2_g), ln2_b=vec(ln2_b),
    )
    pp.update(_s5_tables(s5_a_re, s5_a_im, s5_log_dt, s5_b_re, s5_b_im, s5_c_re, s5_c_im, s5_d))
    del nl
    return pp


def kernel(x, w_in, dn_conv, dn_a_log, dn_dt_bias, dn_norm, w_br_dn, cf_dw, cf_dw_bias, cf_ln_g, cf_ln_b, w_br_cf, s5_a_re, s5_a_im, s5_log_dt, s5_b_re, s5_b_im, s5_c_re, s5_c_im, s5_d, w_br_s5, gla_w_alpha, gla_b_alpha, gla_norm, w_br_gla, w_o, ln1_g, ln1_b, w_up, ffn_conv, w_down, ln2_g, ln2_b):
    bsz, seqlen, d_model = x.shape
    assert d_model == D_MODEL and bsz == SUBLANES
    assert all(seqlen % t == 0 for t in (DN_TILE, SEQ_TILE, MERGE_TILE, FFN_TILE, S5_CHUNK))
    assert (bsz * seqlen) % PROJ_TILE == 0
    depth = w_in.shape[0]
    pp = _pack_params(w_in, dn_conv, dn_a_log, dn_dt_bias, dn_norm, w_br_dn, cf_dw, cf_dw_bias,
                      cf_ln_g, cf_ln_b, w_br_cf, s5_a_re, s5_a_im, s5_log_dt, s5_b_re, s5_b_im,
                      s5_c_re, s5_c_im, s5_d, w_br_s5, gla_w_alpha, gla_b_alpha, gla_norm, w_br_gla,
                      w_o, ln1_g, ln1_b, w_up, ffn_conv, w_down, ln2_g, ln2_b)
    t = bsz * seqlen
    x2 = x.reshape(t, D_MODEL)
    for layer in range(depth):
        (qkv, dng, cf, s5_in, gqk, gv, gg, gates, small, loga) = _proj_call(layer, x2, pp)
        smallt = small[:, :SUBLANES].reshape(t // DN_CHUNK, DN_CHUNK, SUBLANES).transpose(0, 2, 1)
        a_in = _dn_call(layer, bsz, seqlen, qkv, small, smallt, dng, pp)
        d_in = _gla_call(layer, bsz, seqlen, gqk, gv, gg, loga, pp)
        z = _s5_call(layer, bsz, s5_in, pp)
        x2 = _merge_call(layer, bsz, seqlen, x2, a_in, cf, z, d_in, gates, pp)
        x2 = _ffn_call(layer, bsz, seqlen, x2, pp)
    return x2.reshape(bsz, seqlen, D_MODEL)
```

```python
import functools
import math

import jax
import jax.numpy as jnp
from jax import lax
from jax.experimental import pallas as pl
from jax.experimental.pallas import tpu as pltpu

F32 = jnp.float32
BF16 = jnp.bfloat16

D_MODEL = 1024
DN_HEADS = 4
DN_DK = 128
DN_CONV = 4
DN_CHUNK = 64
DN_QKV = 1536
CF_WIDTH = 512
CF_KERNEL = 31
S5_WIDTH = 512
S5_GROUP = 16
S5_GROUPS = 32
S5_STATE = 64
S5_CHUNK = 32
GLA_HEADS = 4
GLA_DK = 64
GLA_DV = 128
GLA_QK = 256
GLA_VW = 512
GLA_RANK = 16
GLA_TAU = 16.0
GLA_CHUNK = 64
GLA_SUB = 16
N_BRANCH = 4
D_FF = 2816
FFN_CONV = 3
FFN_COLS = 256
LN_EPS = 1e-5
DEPTH = 4
DEEPNORM_ALPHA = (2.0 * DEPTH) ** 0.25

LANES = 128
SUBLANES = 8
DN_TILE = 512
SEQ_TILE = 512
MERGE_TILE = 256
FFN_TILE = 256
PROJ_TILE = 512
VMEM_LIMIT = 56 * 1024 * 1024

C_DNG = 0
C_CF = 512
C_S5 = 1536
C_GQK = 2048
C_GV = 2560
C_GG = 3072
W_MID = 3584


def _dot(a, b):
    return jnp.dot(a.astype(BF16), b.astype(BF16), preferred_element_type=F32)


def _dot_nt(a, b):
    return lax.dot_general(a.astype(BF16), b.astype(BF16), (((1,), (1,)), ((), ())),
                           preferred_element_type=F32)


def _dot_tn(a, b):
    return lax.dot_general(a.astype(BF16), b.astype(BF16), (((0,), (0,)), ((), ())),
                           preferred_element_type=F32)


def _split3(x):
    x1 = x.astype(BF16)
    r = x - x1.astype(F32)
    x2 = r.astype(BF16)
    x3 = (r - x2.astype(F32)).astype(BF16)
    return x1, x2, x3


def _split2(x):
    x1 = x.astype(BF16)
    x2 = (x - x1.astype(F32)).astype(BF16)
    return x1, x2


def _dot01_left(m01, x):
    x1, x2, x3 = _split3(x)
    d = lambda v: jnp.dot(m01, v, preferred_element_type=F32)
    return d(x3) + d(x2) + d(x1)


def _dot01_right(x, m01):
    x1, x2, x3 = _split3(x)
    d = lambda v: jnp.dot(v, m01, preferred_element_type=F32)
    return d(x3) + d(x2) + d(x1)


def _dot_hilo(a, b_hi, b_lo):
    a_hi, a_lo = _split2(a)
    d = lambda u, v: jnp.dot(u, v, preferred_element_type=F32)
    return d(a_lo, b_hi) + d(a_hi, b_lo) + d(a_hi, b_hi)


def _sigmoid(x):
    return 1.0 / (1.0 + jnp.exp(-x))


def _silu(x):
    return x * _sigmoid(x)


def _softplus(x):
    return jnp.maximum(x, 0.0) + jnp.log(1.0 + jnp.exp(-jnp.abs(x)))


def _gelu_tanh(x):
    c = math.sqrt(2.0 / math.pi)
    return 0.5 * x * (1.0 + jnp.tanh(c * (x + 0.044715 * (x * x * x))))


def _layer_norm(x, g, b):
    mu = jnp.mean(x, axis=-1, keepdims=True)
    xc = x - mu
    var = jnp.mean(xc * xc, axis=-1, keepdims=True)
    return xc * lax.rsqrt(var + LN_EPS) * g + b


def _tri(n, kind):
    r = lax.broadcasted_iota(jnp.int32, (n, n), 0)
    c = lax.broadcasted_iota(jnp.int32, (n, n), 1)
    if kind == "lower":
        return r >= c
    if kind == "strict":
        return r > c
    if kind == "upper":
        return r <= c
    raise ValueError(kind)


def _const_spec(shape, layer):
    nd = len(shape)
    return pl.BlockSpec((None,) + tuple(shape), lambda *_: (layer,) + (0,) * nd,
                        pipeline_mode=pl.Buffered(1))


def _params(sem):
    return pltpu.CompilerParams(dimension_semantics=sem, vmem_limit_bytes=VMEM_LIMIT)


CF_HALO = 32


def _proj_kernel(tiles_per_seq,
                 x_ref, wq_ref, w_ref, wg_ref, wsh_ref, wsl_ref, vec_ref, walh_ref, wall_ref, bal_ref,
                 dnw_ref, cfw_ref, cfb_ref, cfg_ref, cfbeta_ref,
                 qkv_ref, dng_ref, cfo_ref, s5_ref, gqk_ref, gv_ref, gg_ref, gates_ref,
                 small_ref, loga_ref,
                 dnext_ref, rot_ref, conv_ref):
    tm = x_ref.shape[0]

    @pl.when(pl.program_id(0) % tiles_per_seq == 0)
    def _():
        dnext_ref[0:SUBLANES, :] = jnp.zeros((SUBLANES, DN_QKV), F32)
        rot_ref[0, 0:CF_HALO, :] = jnp.zeros((CF_HALO, CF_WIDTH), F32)

    x = x_ref[...]
    xb = x.astype(BF16)

    def mm(c0, n):
        return jnp.dot(xb, w_ref[:, c0:c0 + n], preferred_element_type=F32)

    for j in range(DN_QKV // 512):
        dnext_ref[SUBLANES:SUBLANES + tm, j * 512:(j + 1) * 512] = jnp.dot(
            xb, wq_ref[:, j * 512:(j + 1) * 512], preferred_element_type=F32)
    cf_a = mm(C_CF, 512)
    cf_g = mm(C_CF + 512, 512)
    rot_ref[0, CF_HALO:CF_HALO + tm, :] = cf_a * _sigmoid(cf_g)

    def dng_step():
        dng_ref[...] = _silu(mm(C_DNG, 512)).astype(BF16)

    def s5_step():
        s5_ref[...] = mm(C_S5, 512).astype(BF16)

    def gq_step():
        gqk_ref[:, 0:GLA_QK] = (mm(C_GQK, GLA_QK) * (GLA_DK ** -0.5)).astype(BF16)
        gqk_ref[:, GLA_QK:2 * GLA_QK] = mm(C_GQK + GLA_QK, GLA_QK).astype(BF16)

    def gv_step():
        gv_ref[...] = mm(C_GV, 512).astype(BF16)

    def gg_step():
        gg_ref[...] = _silu(mm(C_GG, 512)).astype(BF16)

    def gate_step(j):
        gates_ref[:, j * 512:(j + 1) * 512] = _sigmoid(jnp.dot(
            xb, wg_ref[:, j * 512:(j + 1) * 512], preferred_element_type=F32)).astype(BF16)

    def small_step():
        x_lo = (x - xb.astype(F32)).astype(BF16)
        d = lambda u, v: jnp.dot(u, v, preferred_element_type=F32)
        s = d(x_lo, wsh_ref[...]) + d(xb, wsl_ref[...]) + d(xb, wsh_ref[...])
        lane = lax.broadcasted_iota(jnp.int32, s.shape, 1)
        neg_a = vec_ref[0:1, :]
        dt_b = vec_ref[1:2, :]
        g = neg_a * _softplus(s + dt_b)
        beta = _sigmoid(s)
        small_ref[...] = jnp.where(lane < DN_HEADS, g, jnp.where(lane < 2 * DN_HEADS, beta, s))
        z = _dot_hilo(s, walh_ref[...], wall_ref[...]) + bal_ref[...]
        log_sig = jnp.minimum(z, 0.0) - jnp.log(1.0 + jnp.exp(-jnp.abs(z)))
        loga_ref[...] = log_sig * (1.0 / GLA_TAU)

    def dn_conv_step(j):
        sl = slice(j * LANES, (j + 1) * LANES)
        acc = dnw_ref[DN_CONV - 1:DN_CONV, sl] * dnext_ref[SUBLANES:SUBLANES + tm, sl]
        for k in range(DN_CONV - 1):
            off = SUBLANES - (DN_CONV - 1) + k
            acc = acc + dnw_ref[k:k + 1, sl] * dnext_ref[off:off + tm, sl]
        y = _silu(acc)
        if j < 2 * DN_HEADS:
            y = y * lax.rsqrt(jnp.sum(y * y, axis=-1, keepdims=True) + 1e-6)
            if j < DN_HEADS:
                y = y * (DN_DK ** -0.5)
        qkv_ref[:, sl] = y.astype(BF16)

    def cf_shift_step(r):
        nrows = tm + CF_HALO - SUBLANES
        rot_ref[r, 0:nrows, :] = rot_ref[0, r:r + nrows, :]

    def cf_conv_step(j):
        base = CF_HALO - (CF_KERNEL - 1)
        sl = slice(j * LANES, (j + 1) * LANES)
        acc = None
        for k in range(CF_KERNEL):
            r = (base + k) % SUBLANES
            q = base + k - r
            term = cfw_ref[k:k + 1, sl] * rot_ref[r, q:q + tm, sl]
            acc = term if acc is None else acc + term
        conv_ref[:, sl] = acc

    def cf_norm_step():
        rot_ref[0, 0:CF_HALO, :] = rot_ref[0, tm:tm + CF_HALO, :]
        cfo_ref[...] = _silu(_layer_norm(conv_ref[...] + cfb_ref[...], cfg_ref[...],
                                         cfbeta_ref[...])).astype(BF16)

    def dn_tail_step():
        dnext_ref[0:SUBLANES, :] = dnext_ref[tm:tm + SUBLANES, :]

    matmul_steps = ([dng_step, s5_step, gq_step, gv_step, gg_step]
                    + [functools.partial(gate_step, j) for j in range(N_BRANCH * D_MODEL // 512)]
                    + [small_step])
    vector_steps = ([functools.partial(cf_shift_step, r) for r in range(1, SUBLANES)]
                    + [functools.partial(cf_conv_step, j) for j in range(CF_WIDTH // LANES)]
                    + [cf_norm_step]
                    + [functools.partial(dn_conv_step, j) for j in range(DN_QKV // LANES)]
                    + [dn_tail_step])
    for step in matmul_steps + vector_steps:
        step()


def _proj_call(layer, seqlen, x2, pp):
    t = x2.shape[0]
    tm = PROJ_TILE
    row = lambda n: pl.BlockSpec((tm, n), lambda i: (i, 0))
    outs = [(DN_QKV, BF16), (512, BF16), (CF_WIDTH, BF16), (512, BF16), (512, BF16), (512, BF16),
            (512, BF16), (N_BRANCH * D_MODEL, BF16), (LANES, F32), (GLA_QK, F32)]
    return pl.pallas_call(
        functools.partial(_proj_kernel, seqlen // tm),
        grid=(t // tm,),
        in_specs=[row(D_MODEL),
                  _const_spec((D_MODEL, DN_QKV), layer),
                  _const_spec((D_MODEL, W_MID), layer),
                  _const_spec((D_MODEL, N_BRANCH * D_MODEL), layer),
                  _const_spec((D_MODEL, LANES), layer),
                  _const_spec((D_MODEL, LANES), layer),
                  _const_spec((SUBLANES, LANES), layer),
                  _const_spec((LANES, GLA_QK), layer),
                  _const_spec((LANES, GLA_QK), layer),
                  _const_spec((1, GLA_QK), layer),
                  _const_spec((SUBLANES, DN_QKV), layer),
                  _const_spec((CF_HALO, CF_WIDTH), layer), _const_spec((1, CF_WIDTH), layer),
                  _const_spec((1, CF_WIDTH), layer), _const_spec((1, CF_WIDTH), layer)],
        out_specs=[row(n) for n, _ in outs],
        out_shape=[jax.ShapeDtypeStruct((t, n), dt) for n, dt in outs],
        scratch_shapes=[pltpu.VMEM((tm + SUBLANES, DN_QKV), F32),
                        pltpu.VMEM((SUBLANES, tm + CF_HALO, CF_WIDTH), F32),
                        pltpu.VMEM((tm, CF_WIDTH), F32)],
        compiler_params=_params(("arbitrary",)),
        name="proj",
    )(x2, pp["w_qkv"], pp["w_mid"], pp["w_gate"], pp["ws_hi"], pp["ws_lo"], pp["dn_vec"],
      pp["wal_hi"], pp["wal_lo"], pp["b_alpha"], pp["dn_conv"], pp["cf_dw"], pp["cf_dw_bias"],
      pp["cf_ln_g"], pp["cf_ln_b"])


def _dn_kernel(qkv_ref, small_ref, smallt_ref, gate_ref, norm_ref, out_ref, st_ref):
    tl = qkv_ref.shape[0]
    c = DN_CHUNK

    @pl.when(pl.program_id(1) == 0)
    def _():
        st_ref[...] = jnp.zeros(st_ref.shape, F32)

    causal = _tri(c, "lower")
    strict = _tri(c, "strict")
    tril01 = causal.astype(BF16)
    triu01 = _tri(c, "upper").astype(BF16)
    eye = (lax.broadcasted_iota(jnp.int32, (c, c), 0)
           == lax.broadcasted_iota(jnp.int32, (c, c), 1)).astype(F32)
    norm = norm_ref[...]

    def prep_vector(chunks):
        work = []
        for ci in chunks:
            r0 = ci * c
            sm = small_ref[r0:r0 + c, :]
            gc_all = _dot01_left(tril01, sm)
            gcr_all = _dot01_right(smallt_ref[ci], triu01)
            exp_gc = jnp.exp(gc_all)
            g_last = gc_all[c - 1:c, :]
            exp_rem = jnp.exp(g_last - gc_all)
            exp_last = jnp.exp(g_last)
            for h in range(DN_HEADS):
                qb = qkv_ref[r0:r0 + c, h * LANES:(h + 1) * LANES]
                kbf = qkv_ref[r0:r0 + c, (4 + h) * LANES:(5 + h) * LANES]
                q = qb.astype(F32)
                k = kbf.astype(F32)
                v = qkv_ref[r0:r0 + c, (8 + h) * LANES:(9 + h) * LANES].astype(F32)
                beta = sm[:, DN_HEADS + h:DN_HEADS + h + 1]
                diff = gc_all[:, h:h + 1] - gcr_all[h:h + 1, :]
                kb = k * beta
                work.append(dict(
                    r0=r0, h=h, q=qb, k=kbf, kb=kb.astype(BF16),
                    decay=jnp.where(causal, jnp.exp(jnp.where(causal, diff, 0.0)), 0.0),
                    rhs=jnp.concatenate([v * beta, kb * exp_gc[:, h:h + 1]], axis=1).astype(BF16),
                    qd=q * exp_gc[:, h:h + 1], kd=(k * exp_rem[:, h:h + 1]).astype(BF16),
                    last=exp_last[:, h:h + 1]))
        return work

    def solve(work):
        ms = [-jnp.where(strict, _dot_nt(w["kb"], w["k"]) * w["decay"], 0.0) for w in work]
        amat = [(_dot_nt(w["q"], w["k"]) * w["decay"]).astype(BF16) for w in work]
        ps = [eye + m for m in ms]
        for _ in range(int(math.log2(c)) - 1):
            ms = [_dot(m, m) for m in ms]
            ps = [p + _dot(p, m) for p, m in zip(ps, ms)]
        sols = [_dot(p, w["rhs"]) for p, w in zip(ps, work)]
        us = [s[:, :LANES].astype(BF16) for s in sols]
        ws = [s[:, LANES:].astype(BF16) for s in sols]
        for w, a, u, ww in zip(work, amat, us, ws):
            w["qeff"] = w["qd"] - _dot(a, ww)
            w["oc"] = _dot(a, u)
            w["gmat"] = _dot_tn(w["kd"], ww)
            w["bmat"] = _dot_tn(w["kd"], u)

    def apply_state(work):
        for w in work:
            r0, h = w["r0"], w["h"]
            hs = slice(h * LANES, (h + 1) * LANES)
            s = st_ref[h]
            sb = s.astype(BF16)
            o = _dot(w["qeff"], sb) + w["oc"]
            st_ref[h] = s * w["last"] - _dot(w["gmat"], sb) + w["bmat"]
            o = o * lax.rsqrt(jnp.mean(o * o, axis=-1, keepdims=True) + LN_EPS) * norm
            out_ref[r0:r0 + c, hs] = (o * gate_ref[r0:r0 + c, hs].astype(F32)).astype(BF16)

    work = prep_vector(range(tl // c))
    solve(work)
    apply_state(work)


def _dn_call(layer, bsz, seqlen, qkv, small, smallt, gate, pp):
    tl = DN_TILE
    nl = seqlen // tl
    row = lambda n: pl.BlockSpec((tl, n), lambda b, l: (b * nl + l, 0))
    return pl.pallas_call(
        _dn_kernel,
        grid=(bsz, nl),
        in_specs=[row(DN_QKV), row(LANES),
                  pl.BlockSpec((tl // DN_CHUNK, SUBLANES, DN_CHUNK), lambda b, l: (b * nl + l, 0, 0)),
                  row(512),
                  _const_spec((1, LANES), layer)],
        out_specs=row(512),
        out_shape=jax.ShapeDtypeStruct((bsz * seqlen, 512), BF16),
        scratch_shapes=[pltpu.VMEM((DN_HEADS, DN_DK, LANES), F32)],
        compiler_params=_params(("arbitrary", "arbitrary")),
        name="deltanet",
    )(qkv, small, smallt, gate, pp["dn_norm"])


def _gla_kernel(qk_ref, v_ref, g_ref, loga_ref, norm_ref, out_ref, st_ref):
    tl = qk_ref.shape[0]
    c = GLA_CHUNK
    nsub = c // GLA_SUB

    @pl.when(pl.program_id(1) == 0)
    def _():
        st_ref[...] = jnp.zeros(st_ref.shape, F32)

    tril01 = _tri(c, "lower").astype(BF16)
    rows = lax.broadcasted_iota(jnp.int32, (c, LANES), 0)
    lane = lax.broadcasted_iota(jnp.int32, (GLA_SUB, LANES), 1)
    srow = lax.broadcasted_iota(jnp.int32, (GLA_SUB, c), 0)
    scol = lax.broadcasted_iota(jnp.int32, (GLA_SUB, c), 1)
    st_r = lax.broadcasted_iota(jnp.int32, (2 * GLA_DV, LANES), 0) >= GLA_DV
    st_c = lax.broadcasted_iota(jnp.int32, (2 * GLA_DV, LANES), 1) >= GLA_DK
    st_mask = st_r == st_c
    norm = norm_ref[...]

    work = []
    for ci in range(tl // c):
        r0 = ci * c
        gcum = _dot01_left(tril01, loga_ref[r0:r0 + c, :])
        q = qk_ref[r0:r0 + c, 0:GLA_QK].astype(F32)
        k = qk_ref[r0:r0 + c, GLA_QK:2 * GLA_QK].astype(F32)
        g_end = gcum[c - 1:c, :]
        refs = [jnp.zeros((1, GLA_QK), F32)] + [gcum[GLA_SUB * i - 1:GLA_SUB * i, :] for i in range(1, nsub)]
        ref_rows = jnp.concatenate([jnp.broadcast_to(r, (GLA_SUB, GLA_QK)) for r in refs], axis=0)
        qn = q * jnp.exp(gcum - ref_rows)
        qdec = q * jnp.exp(gcum)
        kdec = k * jnp.exp(g_end - gcum)
        for p in range(GLA_HEADS // 2):
            ps = slice(p * LANES, (p + 1) * LANES)
            kp = k[:, ps]
            gp = gcum[:, ps]
            sc = [[], []]
            for i in range(nsub):
                e = jnp.where(rows < GLA_SUB * (i + 1), refs[i][:, ps] - gp, 0.0)
                kn = (kp * jnp.exp(e)).astype(BF16)
                qi = qn[GLA_SUB * i:GLA_SUB * (i + 1), ps]
                for hh in range(2):
                    lhs = jnp.where((lane >= GLA_DK) if hh else (lane < GLA_DK), qi, 0.0)
                    s = _dot_nt(lhs, kn)
                    sc[hh].append(jnp.where(scol <= srow + GLA_SUB * i, s, 0.0))
            vp = v_ref[r0:r0 + c, p * 2 * GLA_DV:(p + 1) * 2 * GLA_DV]
            upd = jnp.where(st_mask, _dot_tn(vp, kdec[:, ps]), 0.0)
            intra = [_dot(jnp.concatenate(sc[hh], axis=0),
                          v_ref[r0:r0 + c, (2 * p + hh) * GLA_DV:(2 * p + hh + 1) * GLA_DV])
                     for hh in range(2)]
            work.append((r0, p, qdec[:, ps].astype(BF16), upd, jnp.exp(g_end[:, ps]), intra))

    for r0, p, qd, upd, dec, intra in work:
        st = st_ref[p]
        o_inter = _dot_nt(qd, st)
        st_ref[p] = st * dec + upd
        for hh in range(2):
            hs = slice((2 * p + hh) * GLA_DV, (2 * p + hh + 1) * GLA_DV)
            o = intra[hh] + o_inter[:, hh * GLA_DV:(hh + 1) * GLA_DV]
            o = o * lax.rsqrt(jnp.mean(o * o, axis=-1, keepdims=True) + LN_EPS) * norm
            out_ref[r0:r0 + c, hs] = (o * g_ref[r0:r0 + c, hs].astype(F32)).astype(BF16)


def _gla_call(layer, bsz, seqlen, gqk, gv, gg, loga, pp):
    tl = SEQ_TILE
    nl = seqlen // tl
    row = lambda n: pl.BlockSpec((tl, n), lambda b, l: (b * nl + l, 0))
    return pl.pallas_call(
        _gla_kernel,
        grid=(bsz, nl),
        in_specs=[row(2 * GLA_QK), row(GLA_VW), row(GLA_VW), row(GLA_QK),
                  _const_spec((1, LANES), layer)],
        out_specs=row(GLA_VW),
        out_shape=jax.ShapeDtypeStruct((bsz * seqlen, GLA_VW), BF16),
        scratch_shapes=[pltpu.VMEM((GLA_HEADS // 2, 2 * GLA_DV, 2 * GLA_DK), F32)],
        compiler_params=_params(("arbitrary", "arbitrary")),
        name="gla",
    )(gqk, gv, gg, loga, pp["gla_norm"])


S5_GPT = LANES // S5_GROUP


def _s5_kernel(u_ref, kc_ref, ptr_ref, pti_ref, qr_ref, qi_ref, ac_ref, y_ref,
               xs_ref, ug_ref, yg_ref, ys_ref, t_ref):
    c = S5_CHUNK
    nch = u_ref.shape[0] // c
    xs_ref[...] = u_ref[...].astype(F32)
    lane = lax.broadcasted_iota(jnp.int32, (nch, LANES), 1)
    piece = [(lane >= S5_GROUP * i) & (lane < S5_GROUP * (i + 1)) for i in range(S5_GPT)]

    for q in range(c // S5_GPT):
        tiles = [None] * S5_GPT
        for i in range(S5_GPT):
            r = xs_ref[pl.ds(S5_GPT * q + i, nch, stride=c), :]
            for g in range(S5_GPT):
                sh = (S5_GROUP * (i - g)) % LANES
                rr = pltpu.roll(r, sh, axis=1) if sh else r
                tiles[g] = jnp.where(piece[i], rr, 0.0 if tiles[g] is None else tiles[g])
        for g in range(S5_GPT):
            ug_ref[g, :, LANES * q:LANES * (q + 1)] = tiles[g].astype(BF16)

    klane = lax.broadcasted_iota(jnp.int32, (S5_GROUP, c * S5_GROUP), 1)
    clane = lax.broadcasted_iota(jnp.int32, (S5_STATE, nch), 1)
    xrs, xis = [], []
    for g in range(S5_GPT):
        kc = kc_ref[g]
        for s in range(c):
            blk = kc if s == 0 else jnp.where(klane >= S5_GROUP * s,
                                              pltpu.roll(kc, S5_GROUP * s, axis=1), 0.0)
            t_ref[g, S5_GROUP * s:S5_GROUP * (s + 1), :] = blk.astype(BF16)
        u = ug_ref[g]
        yg_ref[g] = jnp.dot(u, t_ref[g], preferred_element_type=F32)
        xrs.append(_dot_nt(ptr_ref[g], u))
        xis.append(_dot_nt(pti_ref[g], u))
    prs = [ac_ref[g][:, 0:1] for g in range(S5_GPT)]
    pis = [ac_ref[g][:, 1:2] for g in range(S5_GPT)]
    dist = 1
    while dist < nch:
        for g in range(S5_GPT):
            sr = jnp.where(clane >= dist, pltpu.roll(xrs[g], dist, axis=1), 0.0)
            si = jnp.where(clane >= dist, pltpu.roll(xis[g], dist, axis=1), 0.0)
            pr, pi = prs[g], pis[g]
            xrs[g], xis[g] = xrs[g] + pr * sr - pi * si, xis[g] + pr * si + pi * sr
            prs[g], pis[g] = pr * pr - pi * pi, 2.0 * pr * pi
        dist *= 2
    for g in range(S5_GPT):
        xr = jnp.where(clane >= 1, pltpu.roll(xrs[g], 1, axis=1), 0.0)
        xi = jnp.where(clane >= 1, pltpu.roll(xis[g], 1, axis=1), 0.0)
        yg_ref[g] = yg_ref[g] + _dot_tn(xr, qr_ref[g]) + _dot_tn(xi, qi_ref[g])

    for s in range(c):
        q, i = divmod(s, S5_GPT)
        tile = None
        for g in range(S5_GPT):
            r = yg_ref[g, :, LANES * q:LANES * (q + 1)]
            sh = (S5_GROUP * (g - i)) % LANES
            if sh:
                r = pltpu.roll(r, sh, axis=1)
            tile = jnp.where(piece[g], r, 0.0 if tile is None else tile)
        ys_ref[pl.ds(s, nch, stride=c), :] = tile
    y_ref[...] = ys_ref[...].astype(BF16)


def _s5_call(layer, bsz, u2, pp):
    c = S5_CHUNK
    seqlen = u2.shape[0] // bsz
    nch = seqlen // c
    assert nch <= LANES, "the chunk scan keeps one sequence's chunks inside one lane tile"
    tab = lambda a, b: pl.BlockSpec((None, S5_GPT, a, b), lambda bi, j: (layer, j, 0, 0))
    io = pl.BlockSpec((seqlen, LANES), lambda bi, j: (bi, j))
    return pl.pallas_call(
        _s5_kernel,
        grid=(bsz, u2.shape[1] // LANES),
        in_specs=[io, tab(S5_GROUP, c * S5_GROUP), tab(S5_STATE, c * S5_GROUP),
                  tab(S5_STATE, c * S5_GROUP), tab(S5_STATE, c * S5_GROUP),
                  tab(S5_STATE, c * S5_GROUP), tab(S5_STATE, LANES)],
        out_specs=io,
        out_shape=jax.ShapeDtypeStruct(u2.shape, BF16),
        scratch_shapes=[pltpu.VMEM((seqlen, LANES), F32),
                        pltpu.VMEM((S5_GPT, nch, c * S5_GROUP), BF16),
                        pltpu.VMEM((S5_GPT, nch, c * S5_GROUP), F32),
                        pltpu.VMEM((seqlen, LANES), F32),
                        pltpu.VMEM((S5_GPT, c * S5_GROUP, c * S5_GROUP), BF16)],
        compiler_params=_params(("arbitrary", "arbitrary")),
        name="s5",
    )(u2, pp["s5_kc"], pp["s5_ptr"], pp["s5_pti"], pp["s5_qr"], pp["s5_qi"], pp["s5_ac"])


def _merge_kernel(x_ref, a_ref, cfo_ref, z_ref, d_ref, gates_ref,
                  wdn_ref, wcf_ref, ws5_ref, wgla_ref, wo_ref, lng_ref, lnb_ref, out_ref):
    gt = lambda i: gates_ref[:, i * D_MODEL:(i + 1) * D_MODEL].astype(F32)
    y_a = jnp.dot(a_ref[...], wdn_ref[...], preferred_element_type=F32)
    y_b = jnp.dot(cfo_ref[...], wcf_ref[...], preferred_element_type=F32)
    y_d = jnp.dot(d_ref[...], wgla_ref[...], preferred_element_type=F32)
    zg = _gelu_tanh(z_ref[...].astype(F32)).astype(BF16)
    z_val = jnp.dot(zg, ws5_ref[:, 0:D_MODEL], preferred_element_type=F32)
    z_gate = jnp.dot(zg, ws5_ref[:, D_MODEL:2 * D_MODEL], preferred_element_type=F32)
    y_c = z_val * _sigmoid(z_gate)
    merged = gt(0) * y_a + gt(1) * y_b + gt(2) * y_c + gt(3) * y_d
    mix = jnp.dot(merged.astype(BF16), wo_ref[...], preferred_element_type=F32)
    out_ref[...] = _layer_norm(DEEPNORM_ALPHA * x_ref[...] + mix, lng_ref[...], lnb_ref[...])


def _merge_call(layer, bsz, seqlen, x2, a_in, cf, z, d_in, gates, pp):
    tl = MERGE_TILE
    nl = seqlen // tl
    row = lambda n: pl.BlockSpec((tl, n), lambda b, l: (b * nl + l, 0))
    return pl.pallas_call(
        _merge_kernel,
        grid=(bsz, nl),
        in_specs=[row(D_MODEL), row(512), row(CF_WIDTH), row(S5_WIDTH), row(GLA_VW),
                  row(N_BRANCH * D_MODEL),
                  _const_spec((512, D_MODEL), layer), _const_spec((CF_WIDTH, D_MODEL), layer),
                  _const_spec((S5_WIDTH, 2 * D_MODEL), layer), _const_spec((GLA_VW, D_MODEL), layer),
                  _const_spec((D_MODEL, D_MODEL), layer),
                  _const_spec((1, D_MODEL), layer), _const_spec((1, D_MODEL), layer)],
        out_specs=row(D_MODEL),
        out_shape=jax.ShapeDtypeStruct((bsz * seqlen, D_MODEL), F32),
        compiler_params=_params(("arbitrary", "arbitrary")),
        name="merge",
    )(x2, a_in, cf, z, d_in, gates, pp["w_br_dn"], pp["w_br_cf"], pp["w_br_s5"], pp["w_br_gla"],
      pp["w_o"], pp["ln1_g"], pp["ln1_b"])


def _ffn_kernel(x_ref, wup_ref, convw_ref, wdown_ref, lng_ref, lnb_ref, out_ref, ext_ref, hid_ref):
    tl = x_ref.shape[0]
    w = FFN_COLS

    @pl.when(pl.program_id(1) == 0)
    def _():
        ext_ref[0:SUBLANES, :] = jnp.zeros((SUBLANES, 2 * D_FF), F32)

    x = x_ref[...]
    xb = x.astype(BF16)
    nchunks = D_FF // w
    cols = lambda ci, half: slice(half * D_FF + ci * w, half * D_FF + (ci + 1) * w)

    def up(ci):
        us = []
        for half in range(2):
            u = jnp.dot(xb, wup_ref[:, cols(ci, half)], preferred_element_type=F32)
            ext_ref[SUBLANES:SUBLANES + tl, cols(ci, half)] = u
            us.append(u)
        return us

    us = up(0)
    for ci in range(nchunks):
        nxt = up(ci + 1) if ci + 1 < nchunks else None
        halves = []
        for half in range(2):
            cs = cols(ci, half)
            y = convw_ref[FFN_CONV - 1:FFN_CONV, cs] * us[half]
            for k in range(FFN_CONV - 1):
                off = SUBLANES - (FFN_CONV - 1) + k
                y = y + convw_ref[k:k + 1, cs] * ext_ref[off:off + tl, cs]
            halves.append(y)
        hid_ref[:, ci * w:(ci + 1) * w] = (_silu(halves[0]) * halves[1]).astype(BF16)
        us = nxt
    acc = jnp.dot(hid_ref[...], wdown_ref[...], preferred_element_type=F32)
    ext_ref[0:SUBLANES, :] = ext_ref[tl:tl + SUBLANES, :]
    out_ref[...] = _layer_norm(DEEPNORM_ALPHA * x + acc, lng_ref[...], lnb_ref[...])


def _ffn_call(layer, bsz, seqlen, x2, pp):
    tl = FFN_TILE
    nl = seqlen // tl
    row = lambda n: pl.BlockSpec((tl, n), lambda b, l: (b * nl + l, 0))
    return pl.pallas_call(
        _ffn_kernel,
        grid=(bsz, nl),
        in_specs=[row(D_MODEL),
                  _const_spec((D_MODEL, 2 * D_FF), layer), _const_spec((SUBLANES, 2 * D_FF), layer),
                  _const_spec((D_FF, D_MODEL), layer),
                  _const_spec((1, D_MODEL), layer), _const_spec((1, D_MODEL), layer)],
        out_specs=row(D_MODEL),
        out_shape=jax.ShapeDtypeStruct((bsz * seqlen, D_MODEL), F32),
        scratch_shapes=[pltpu.VMEM((tl + SUBLANES, 2 * D_FF), F32),
                        pltpu.VMEM((tl, D_FF), BF16)],
        compiler_params=_params(("arbitrary", "arbitrary")),
        name="convffn",
    )(x2, pp["w_up"], pp["ffn_conv"], pp["w_down"], pp["ln2_g"], pp["ln2_b"])


def _pad_rows(a, rows):
    return jnp.pad(a, ((0, 0), (0, rows - a.shape[1]), (0, 0)))


def _s5_tables(a_re, a_im, log_dt, b_re, b_im, c_re, c_im, d):
    c = S5_CHUNK
    dt = jnp.exp(log_dt)[..., None]
    mag = jnp.exp(dt * a_re)
    abar_re, abar_im = mag * jnp.cos(dt * a_im), mag * jnp.sin(dt * a_im)
    den = a_re * a_re + a_im * a_im
    nr, ni = abar_re - 1.0, abar_im
    fr, fi = (nr * a_re + ni * a_im) / den, (ni * a_re - nr * a_im) / den
    bb_re = fr[..., None] * b_re - fi[..., None] * b_im
    bb_im = fr[..., None] * b_im + fi[..., None] * b_re
    j = jnp.arange(c + 1, dtype=F32)[:, None, None, None]
    pmag = jnp.exp(j * (dt * a_re)[None])
    pw_re = pmag * jnp.cos(j * (dt * a_im)[None])
    pw_im = pmag * jnp.sin(j * (dt * a_im)[None])
    cb_re = jnp.einsum('lgon,lgni->lgnoi', c_re, bb_re) - jnp.einsum('lgon,lgni->lgnoi', c_im, bb_im)
    cb_im = jnp.einsum('lgon,lgni->lgnoi', c_re, bb_im) + jnp.einsum('lgon,lgni->lgnoi', c_im, bb_re)
    kern = (jnp.einsum('jlgn,lgnoi->jlgoi', pw_re[:c], cb_re)
            - jnp.einsum('jlgn,lgnoi->jlgoi', pw_im[:c], cb_im))
    eye = jnp.eye(S5_GROUP, dtype=F32)
    kern = kern.at[0].add(d.reshape(d.shape[0], S5_GROUPS, S5_GROUP)[..., None] * eye)
    kcat = kern.transpose(1, 2, 4, 0, 3).reshape(d.shape[0], S5_GROUPS, S5_GROUP, c * S5_GROUP)
    rp_re, rp_im = pw_re[:c][::-1], pw_im[:c][::-1]
    p_re = rp_re[..., None] * bb_re[None] - rp_im[..., None] * bb_im[None]
    p_im = rp_re[..., None] * bb_im[None] + rp_im[..., None] * bb_re[None]
    tos = lambda a: a.transpose(1, 2, 3, 0, 4).reshape(d.shape[0], S5_GROUPS, S5_STATE, c * S5_GROUP)
    q_re = (jnp.einsum('lgon,tlgn->lgnto', c_re, pw_re[1:]) - jnp.einsum('lgon,tlgn->lgnto', c_im, pw_im[1:]))
    q_im = -(jnp.einsum('lgon,tlgn->lgnto', c_re, pw_im[1:]) + jnp.einsum('lgon,tlgn->lgnto', c_im, pw_re[1:]))
    toq = lambda a: a.reshape(d.shape[0], S5_GROUPS, S5_STATE, c * S5_GROUP)
    ac = jnp.stack([pw_re[c], pw_im[c]], axis=-1)
    ac = jnp.pad(ac, ((0, 0), (0, 0), (0, 0), (0, LANES - 2)))
    return dict(s5_kc=kcat, s5_ptr=tos(p_re).astype(BF16), s5_pti=tos(p_im).astype(BF16),
                s5_qr=toq(q_re).astype(BF16), s5_qi=toq(q_im).astype(BF16), s5_ac=ac)


def _pack_params(w_in, dn_conv, dn_a_log, dn_dt_bias, dn_norm, w_br_dn, cf_dw, cf_dw_bias, cf_ln_g,
                 cf_ln_b, w_br_cf, s5_a_re, s5_a_im, s5_log_dt, s5_b_re, s5_b_im, s5_c_re, s5_c_im,
                 s5_d, w_br_s5, gla_w_alpha, gla_b_alpha, gla_norm, w_br_gla, w_o, ln1_g, ln1_b,
                 w_up, ffn_conv, w_down, ln2_g, ln2_b):
    sizes = (DN_QKV, DN_HEADS, DN_HEADS, 512, 2 * CF_WIDTH, S5_WIDTH, GLA_QK, GLA_QK, GLA_VW, GLA_VW,
             GLA_RANK, N_BRANCH * D_MODEL)
    offs = [0]
    for s in sizes:
        offs.append(offs[-1] + s)
    col = lambda i: w_in[:, :, offs[i]:offs[i + 1]]
    w_small = jnp.concatenate([col(1), col(2), col(10)], axis=-1)
    w_small = jnp.pad(w_small, ((0, 0), (0, 0), (0, LANES - w_small.shape[-1])))
    ws_hi = w_small.astype(BF16)
    ws_lo = (w_small - ws_hi.astype(F32)).astype(BF16)
    pad_l = lambda a: jnp.pad(a, ((0, 0), (0, LANES - a.shape[-1])))
    dn_vec = jnp.stack([pad_l(-jnp.exp(dn_a_log)), pad_l(dn_dt_bias)], axis=1)
    dn_vec = _pad_rows(dn_vec, SUBLANES)
    wal = jnp.pad(gla_w_alpha, ((0, 0), (2 * DN_HEADS, LANES - 2 * DN_HEADS - GLA_RANK), (0, 0)))
    wal_hi = wal.astype(BF16)
    wal_lo = (wal - wal_hi.astype(F32)).astype(BF16)
    vec = lambda a: a[:, None, :]
    pp = dict(
        w_qkv=col(0).astype(BF16), w_mid=w_in[:, :, offs[3]:offs[10]].astype(BF16),
        w_gate=col(11).astype(BF16), ws_hi=ws_hi, ws_lo=ws_lo, dn_vec=dn_vec, wal_hi=wal_hi, wal_lo=wal_lo,
        b_alpha=vec(gla_b_alpha),
        dn_conv=_pad_rows(dn_conv, SUBLANES), dn_norm=vec(dn_norm), gla_norm=vec(gla_norm),
        w_br_dn=w_br_dn.astype(BF16), w_br_cf=w_br_cf.astype(BF16), w_br_s5=w_br_s5.astype(BF16),
        w_br_gla=w_br_gla.astype(BF16), w_o=w_o.astype(BF16),
        cf_dw=_pad_rows(cf_dw, CF_HALO), cf_dw_bias=vec(cf_dw_bias), cf_ln_g=vec(cf_ln_g),
        cf_ln_b=vec(cf_ln_b), ln1_g=vec(ln1_g), ln1_b=vec(ln1_b),
        w_up=w_up.astype(BF16), ffn_conv=_pad_rows(ffn_conv, SUBLANES), w_down=w_down.astype(BF16),
        ln2_g=vec(ln2_g), ln2_b=vec(ln2_b),
    )
    pp.update(_s5_tables(s5_a_re, s5_a_im, s5_log_dt, s5_b_re, s5_b_im, s5_c_re, s5_c_im, s5_d))
    return pp


def kernel(x, w_in, dn_conv, dn_a_log, dn_dt_bias, dn_norm, w_br_dn, cf_dw, cf_dw_bias, cf_ln_g, cf_ln_b, w_br_cf, s5_a_re, s5_a_im, s5_log_dt, s5_b_re, s5_b_im, s5_c_re, s5_c_im, s5_d, w_br_s5, gla_w_alpha, gla_b_alpha, gla_norm, w_br_gla, w_o, ln1_g, ln1_b, w_up, ffn_conv, w_down, ln2_g, ln2_b):
    bsz, seqlen, d_model = x.shape
    assert d_model == D_MODEL and bsz == SUBLANES
    assert all(seqlen % t == 0 for t in (DN_TILE, SEQ_TILE, MERGE_TILE, FFN_TILE, S5_CHUNK))
    assert seqlen % PROJ_TILE == 0
    depth = w_in.shape[0]
    pp = _pack_params(w_in, dn_conv, dn_a_log, dn_dt_bias, dn_norm, w_br_dn, cf_dw, cf_dw_bias,
                      cf_ln_g, cf_ln_b, w_br_cf, s5_a_re, s5_a_im, s5_log_dt, s5_b_re, s5_b_im,
                      s5_c_re, s5_c_im, s5_d, w_br_s5, gla_w_alpha, gla_b_alpha, gla_norm, w_br_gla,
                      w_o, ln1_g, ln1_b, w_up, ffn_conv, w_down, ln2_g, ln2_b)
    t = bsz * seqlen
    x2 = x.reshape(t, D_MODEL)
    for layer in range(depth):
        (qkv, dng, cf, s5_in, gqk, gv, gg, gates, small, loga) = _proj_call(layer, seqlen, x2, pp)
        smallt = small[:, :SUBLANES].reshape(t // DN_CHUNK, DN_CHUNK, SUBLANES).transpose(0, 2, 1)
        a_in = _dn_call(layer, bsz, seqlen, qkv, small, smallt, dng, pp)
        d_in = _gla_call(layer, bsz, seqlen, gqk, gv, gg, loga, pp)
        z = _s5_call(layer, bsz, s5_in, pp)
        x2 = _merge_call(layer, bsz, seqlen, x2, a_in, cf, z, d_in, gates, pp)
        x2 = _ffn_call(layer, bsz, seqlen, x2, pp)
    return x2.reshape(bsz, seqlen, D_MODEL)
```

```python
import functools
import math

import jax
import jax.numpy as jnp
from jax import lax
from jax.experimental import pallas as pl
from jax.experimental.pallas import tpu as pltpu

F32 = jnp.float32
BF16 = jnp.bfloat16

D_MODEL = 1024
DN_HEADS = 4
DN_DK = 128
DN_CONV = 4
DN_CHUNK = 64
DN_QKV = 1536
CF_WIDTH = 512
CF_KERNEL = 31
S5_WIDTH = 512
S5_GROUP = 16
S5_GROUPS = 32
S5_STATE = 64
S5_CHUNK = 32
GLA_HEADS = 4
GLA_DK = 64
GLA_DV = 128
GLA_QK = 256
GLA_VW = 512
GLA_RANK = 16
GLA_TAU = 16.0
GLA_CHUNK = 64
GLA_SUB = 16
N_BRANCH = 4
D_FF = 2816
FFN_CONV = 3
FFN_COLS = 256
LN_EPS = 1e-5
DEPTH = 4
DEEPNORM_ALPHA = (2.0 * DEPTH) ** 0.25

LANES = 128
SUBLANES = 8
DN_TILE = 512
SEQ_TILE = 512
MERGE_TILE = 512
FFN_TILE = 512
PROJ_TILE = 512
VMEM_LIMIT = 56 * 1024 * 1024

C_DNG = 0
C_CF = 512
C_S5 = 1536
C_GQK = 2048
C_GV = 2560
C_GG = 3072
W_MID = 3584


def _dot(a, b):
    return jnp.dot(a.astype(BF16), b.astype(BF16), preferred_element_type=F32)


def _dot_nt(a, b):
    return lax.dot_general(a.astype(BF16), b.astype(BF16), (((1,), (1,)), ((), ())),
                           preferred_element_type=F32)


def _dot_tn(a, b):
    return lax.dot_general(a.astype(BF16), b.astype(BF16), (((0,), (0,)), ((), ())),
                           preferred_element_type=F32)


def _split3(x):
    x1 = x.astype(BF16)
    r = x - x1.astype(F32)
    x2 = r.astype(BF16)
    x3 = (r - x2.astype(F32)).astype(BF16)
    return x1, x2, x3


def _split2(x):
    x1 = x.astype(BF16)
    x2 = (x - x1.astype(F32)).astype(BF16)
    return x1, x2


def _dot01_left(m01, x):
    x1, x2, x3 = _split3(x)
    d = lambda v: jnp.dot(m01, v, preferred_element_type=F32)
    return d(x3) + d(x2) + d(x1)


def _dot01_right(x, m01):
    x1, x2, x3 = _split3(x)
    d = lambda v: jnp.dot(v, m01, preferred_element_type=F32)
    return d(x3) + d(x2) + d(x1)


def _dot_hilo(a, b_hi, b_lo):
    a_hi, a_lo = _split2(a)
    d = lambda u, v: jnp.dot(u, v, preferred_element_type=F32)
    return d(a_lo, b_hi) + d(a_hi, b_lo) + d(a_hi, b_hi)


def _sigmoid(x):
    return 1.0 / (1.0 + jnp.exp(-x))


def _silu(x):
    return x * _sigmoid(x)


def _softplus(x):
    return jnp.maximum(x, 0.0) + jnp.log(1.0 + jnp.exp(-jnp.abs(x)))


def _gelu_tanh(x):
    c = math.sqrt(2.0 / math.pi)
    return 0.5 * x * (1.0 + jnp.tanh(c * (x + 0.044715 * (x * x * x))))


def _layer_norm(x, g, b):
    mu = jnp.mean(x, axis=-1, keepdims=True)
    xc = x - mu
    var = jnp.mean(xc * xc, axis=-1, keepdims=True)
    return xc * lax.rsqrt(var + LN_EPS) * g + b


def _tri(n, kind):
    r = lax.broadcasted_iota(jnp.int32, (n, n), 0)
    c = lax.broadcasted_iota(jnp.int32, (n, n), 1)
    if kind == "lower":
        return r >= c
    if kind == "strict":
        return r > c
    if kind == "upper":
        return r <= c
    raise ValueError(kind)


def _const_spec(shape, layer):
    nd = len(shape)
    return pl.BlockSpec((None,) + tuple(shape), lambda *_: (layer,) + (0,) * nd,
                        pipeline_mode=pl.Buffered(1))


def _params(sem):
    return pltpu.CompilerParams(dimension_semantics=sem, vmem_limit_bytes=VMEM_LIMIT)


CF_HALO = 32


def _proj_kernel(tiles_per_seq,
                 x_ref, wq_ref, w_ref, wg_ref, wsh_ref, wsl_ref, vec_ref, walh_ref, wall_ref, bal_ref,
                 dnw_ref, cfw_ref, cfb_ref, cfg_ref, cfbeta_ref,
                 qkv_ref, dng_ref, cfo_ref, s5_ref, gqk_ref, gv_ref, gg_ref, gates_ref,
                 small_ref, loga_ref,
                 dnext_ref, rot_ref, conv_ref):
    tm = x_ref.shape[0]

    @pl.when(pl.program_id(0) % tiles_per_seq == 0)
    def _():
        dnext_ref[0:SUBLANES, :] = jnp.zeros((SUBLANES, DN_QKV), F32)
        rot_ref[0, 0:CF_HALO, :] = jnp.zeros((CF_HALO, CF_WIDTH), F32)

    x = x_ref[...]
    xb = x.astype(BF16)

    def mm(c0, n):
        return jnp.dot(xb, w_ref[:, c0:c0 + n], preferred_element_type=F32)

    for j in range(DN_QKV // 512):
        dnext_ref[SUBLANES:SUBLANES + tm, j * 512:(j + 1) * 512] = jnp.dot(
            xb, wq_ref[:, j * 512:(j + 1) * 512], preferred_element_type=F32)
    cf_a = mm(C_CF, 512)
    cf_g = mm(C_CF + 512, 512)
    rot_ref[0, CF_HALO:CF_HALO + tm, :] = cf_a * _sigmoid(cf_g)

    def dng_step():
        dng_ref[...] = _silu(mm(C_DNG, 512)).astype(BF16)

    def s5_step():
        s5_ref[...] = mm(C_S5, 512).astype(BF16)

    def gq_step():
        gqk_ref[:, 0:GLA_QK] = (mm(C_GQK, GLA_QK) * (GLA_DK ** -0.5)).astype(BF16)
        gqk_ref[:, GLA_QK:2 * GLA_QK] = mm(C_GQK + GLA_QK, GLA_QK).astype(BF16)

    def gv_step():
        gv_ref[...] = mm(C_GV, 512).astype(BF16)

    def gg_step():
        gg_ref[...] = _silu(mm(C_GG, 512)).astype(BF16)

    def gate_step(j):
        gates_ref[:, j * 512:(j + 1) * 512] = _sigmoid(jnp.dot(
            xb, wg_ref[:, j * 512:(j + 1) * 512], preferred_element_type=F32)).astype(BF16)

    def small_step():
        x_lo = (x - xb.astype(F32)).astype(BF16)
        d = lambda u, v: jnp.dot(u, v, preferred_element_type=F32)
        s = d(x_lo, wsh_ref[...]) + d(xb, wsl_ref[...]) + d(xb, wsh_ref[...])
        lane = lax.broadcasted_iota(jnp.int32, s.shape, 1)
        neg_a = vec_ref[0:1, :]
        dt_b = vec_ref[1:2, :]
        g = neg_a * _softplus(s + dt_b)
        beta = _sigmoid(s)
        small_ref[...] = jnp.where(lane < DN_HEADS, g, jnp.where(lane < 2 * DN_HEADS, beta, s))
        z = _dot_hilo(s, walh_ref[...], wall_ref[...]) + bal_ref[...]
        log_sig = jnp.minimum(z, 0.0) - jnp.log(1.0 + jnp.exp(-jnp.abs(z)))
        loga_ref[...] = log_sig * (1.0 / GLA_TAU)

    def dn_conv_step(j):
        sl = slice(j * LANES, (j + 1) * LANES)
        acc = dnw_ref[DN_CONV - 1:DN_CONV, sl] * dnext_ref[SUBLANES:SUBLANES + tm, sl]
        for k in range(DN_CONV - 1):
            off = SUBLANES - (DN_CONV - 1) + k
            acc = acc + dnw_ref[k:k + 1, sl] * dnext_ref[off:off + tm, sl]
        y = _silu(acc)
        if j < 2 * DN_HEADS:
            y = y * lax.rsqrt(jnp.sum(y * y, axis=-1, keepdims=True) + 1e-6)
            if j < DN_HEADS:
                y = y * (DN_DK ** -0.5)
        qkv_ref[:, sl] = y.astype(BF16)

    def cf_shift_step(r):
        nrows = tm + CF_HALO - SUBLANES
        rot_ref[r, 0:nrows, :] = rot_ref[0, r:r + nrows, :]

    def cf_conv_step(j):
        base = CF_HALO - (CF_KERNEL - 1)
        sl = slice(j * LANES, (j + 1) * LANES)
        acc = None
        for k in range(CF_KERNEL):
            r = (base + k) % SUBLANES
            q = base + k - r
            term = cfw_ref[k:k + 1, sl] * rot_ref[r, q:q + tm, sl]
            acc = term if acc is None else acc + term
        conv_ref[:, sl] = acc

    def cf_norm_step():
        rot_ref[0, 0:CF_HALO, :] = rot_ref[0, tm:tm + CF_HALO, :]
        cfo_ref[...] = _silu(_layer_norm(conv_ref[...] + cfb_ref[...], cfg_ref[...],
                                         cfbeta_ref[...])).astype(BF16)

    def dn_tail_step():
        dnext_ref[0:SUBLANES, :] = dnext_ref[tm:tm + SUBLANES, :]

    matmul_steps = ([dng_step, s5_step, gq_step, gv_step, gg_step]
                    + [functools.partial(gate_step, j) for j in range(N_BRANCH * D_MODEL // 512)]
                    + [small_step])
    vector_steps = ([functools.partial(cf_shift_step, r) for r in range(1, SUBLANES)]
                    + [functools.partial(cf_conv_step, j) for j in range(CF_WIDTH // LANES)]
                    + [cf_norm_step]
                    + [functools.partial(dn_conv_step, j) for j in range(DN_QKV // LANES)]
                    + [dn_tail_step])
    for step in matmul_steps + vector_steps:
        step()


def _proj_call(layer, seqlen, x2, pp):
    t = x2.shape[0]
    tm = PROJ_TILE
    row = lambda n: pl.BlockSpec((tm, n), lambda i: (i, 0))
    outs = [(DN_QKV, BF16), (512, BF16), (CF_WIDTH, BF16), (512, BF16), (512, BF16), (512, BF16),
            (512, BF16), (N_BRANCH * D_MODEL, BF16), (LANES, F32), (GLA_QK, F32)]
    return pl.pallas_call(
        functools.partial(_proj_kernel, seqlen // tm),
        grid=(t // tm,),
        in_specs=[row(D_MODEL),
                  _const_spec((D_MODEL, DN_QKV), layer),
                  _const_spec((D_MODEL, W_MID), layer),
                  _const_spec((D_MODEL, N_BRANCH * D_MODEL), layer),
                  _const_spec((D_MODEL, LANES), layer),
                  _const_spec((D_MODEL, LANES), layer),
                  _const_spec((SUBLANES, LANES), layer),
                  _const_spec((LANES, GLA_QK), layer),
                  _const_spec((LANES, GLA_QK), layer),
                  _const_spec((1, GLA_QK), layer),
                  _const_spec((SUBLANES, DN_QKV), layer),
                  _const_spec((CF_HALO, CF_WIDTH), layer), _const_spec((1, CF_WIDTH), layer),
                  _const_spec((1, CF_WIDTH), layer), _const_spec((1, CF_WIDTH), layer)],
        out_specs=[row(n) for n, _ in outs],
        out_shape=[jax.ShapeDtypeStruct((t, n), dt) for n, dt in outs],
        scratch_shapes=[pltpu.VMEM((tm + SUBLANES, DN_QKV), F32),
                        pltpu.VMEM((SUBLANES, tm + CF_HALO, CF_WIDTH), F32),
                        pltpu.VMEM((tm, CF_WIDTH), F32)],
        compiler_params=_params(("arbitrary",)),
        name="proj",
    )(x2, pp["w_qkv"], pp["w_mid"], pp["w_gate"], pp["ws_hi"], pp["ws_lo"], pp["dn_vec"],
      pp["wal_hi"], pp["wal_lo"], pp["b_alpha"], pp["dn_conv"], pp["cf_dw"], pp["cf_dw_bias"],
      pp["cf_ln_g"], pp["cf_ln_b"])


def _dn_kernel(qkv_ref, small_ref, smallt_ref, gate_ref, norm_ref, out_ref, st_ref):
    tl = qkv_ref.shape[0]
    c = DN_CHUNK

    @pl.when(pl.program_id(1) == 0)
    def _():
        st_ref[...] = jnp.zeros(st_ref.shape, F32)

    causal = _tri(c, "lower")
    strict = _tri(c, "strict")
    tril01 = causal.astype(BF16)
    triu01 = _tri(c, "upper").astype(BF16)
    eye = (lax.broadcasted_iota(jnp.int32, (c, c), 0)
           == lax.broadcasted_iota(jnp.int32, (c, c), 1)).astype(F32)
    norm = norm_ref[...]

    def prep_vector(chunks):
        work = []
        for ci in chunks:
            r0 = ci * c
            sm = small_ref[r0:r0 + c, :]
            gc_all = _dot01_left(tril01, sm)
            gcr_all = _dot01_right(smallt_ref[ci], triu01)
            exp_gc = jnp.exp(gc_all)
            g_last = gc_all[c - 1:c, :]
            exp_rem = jnp.exp(g_last - gc_all)
            exp_last = jnp.exp(g_last)
            for h in range(DN_HEADS):
                qb = qkv_ref[r0:r0 + c, h * LANES:(h + 1) * LANES]
                kbf = qkv_ref[r0:r0 + c, (4 + h) * LANES:(5 + h) * LANES]
                q = qb.astype(F32)
                k = kbf.astype(F32)
                v = qkv_ref[r0:r0 + c, (8 + h) * LANES:(9 + h) * LANES].astype(F32)
                beta = sm[:, DN_HEADS + h:DN_HEADS + h + 1]
                diff = gc_all[:, h:h + 1] - gcr_all[h:h + 1, :]
                kb = k * beta
                work.append(dict(
                    r0=r0, h=h, q=qb, k=kbf, kb=kb.astype(BF16),
                    decay=jnp.where(causal, jnp.exp(jnp.where(causal, diff, 0.0)), 0.0),
                    rhs=jnp.concatenate([v * beta, kb * exp_gc[:, h:h + 1]], axis=1).astype(BF16),
                    qd=q * exp_gc[:, h:h + 1], kd=(k * exp_rem[:, h:h + 1]).astype(BF16),
                    last=exp_last[:, h:h + 1]))
        return work

    def solve(work):
        ms = [-jnp.where(strict, _dot_nt(w["kb"], w["k"]) * w["decay"], 0.0) for w in work]
        amat = [(_dot_nt(w["q"], w["k"]) * w["decay"]).astype(BF16) for w in work]
        ps = [eye + m for m in ms]
        for _ in range(int(math.log2(c)) - 1):
            ms = [_dot(m, m) for m in ms]
            ps = [p + _dot(p, m) for p, m in zip(ps, ms)]
        sols = [_dot(p, w["rhs"]) for p, w in zip(ps, work)]
        us = [s[:, :LANES].astype(BF16) for s in sols]
        ws = [s[:, LANES:].astype(BF16) for s in sols]
        for w, a, u, ww in zip(work, amat, us, ws):
            w["qeff"] = w["qd"] - _dot(a, ww)
            w["oc"] = _dot(a, u)
            w["gmat"] = _dot_tn(w["kd"], ww)
            w["bmat"] = _dot_tn(w["kd"], u)

    def apply_state(work):
        for w in work:
            r0, h = w["r0"], w["h"]
            hs = slice(h * LANES, (h + 1) * LANES)
            s = st_ref[h]
            sb = s.astype(BF16)
            o = _dot(w["qeff"], sb) + w["oc"]
            st_ref[h] = s * w["last"] - _dot(w["gmat"], sb) + w["bmat"]
            o = o * lax.rsqrt(jnp.mean(o * o, axis=-1, keepdims=True) + LN_EPS) * norm
            out_ref[r0:r0 + c, hs] = (o * gate_ref[r0:r0 + c, hs].astype(F32)).astype(BF16)

    work = prep_vector(range(tl // c))
    solve(work)
    apply_state(work)


def _dn_call(layer, bsz, seqlen, qkv, small, smallt, gate, pp):
    tl = DN_TILE
    nl = seqlen // tl
    row = lambda n: pl.BlockSpec((tl, n), lambda b, l: (b * nl + l, 0))
    return pl.pallas_call(
        _dn_kernel,
        grid=(bsz, nl),
        in_specs=[row(DN_QKV), row(LANES),
                  pl.BlockSpec((tl // DN_CHUNK, SUBLANES, DN_CHUNK), lambda b, l: (b * nl + l, 0, 0)),
                  row(512),
                  _const_spec((1, LANES), layer)],
        out_specs=row(512),
        out_shape=jax.ShapeDtypeStruct((bsz * seqlen, 512), BF16),
        scratch_shapes=[pltpu.VMEM((DN_HEADS, DN_DK, LANES), F32)],
        compiler_params=_params(("arbitrary", "arbitrary")),
        name="deltanet",
    )(qkv, small, smallt, gate, pp["dn_norm"])


def _gla_kernel(qk_ref, v_ref, g_ref, loga_ref, norm_ref, out_ref, st_ref):
    tl = qk_ref.shape[0]
    c = GLA_CHUNK
    nsub = c // GLA_SUB

    @pl.when(pl.program_id(1) == 0)
    def _():
        st_ref[...] = jnp.zeros(st_ref.shape, F32)

    tril01 = _tri(c, "lower").astype(BF16)
    rows = lax.broadcasted_iota(jnp.int32, (c, LANES), 0)
    lane = lax.broadcasted_iota(jnp.int32, (GLA_SUB, LANES), 1)
    srow = lax.broadcasted_iota(jnp.int32, (GLA_SUB, c), 0)
    scol = lax.broadcasted_iota(jnp.int32, (GLA_SUB, c), 1)
    st_r = lax.broadcasted_iota(jnp.int32, (2 * GLA_DV, LANES), 0) >= GLA_DV
    st_c = lax.broadcasted_iota(jnp.int32, (2 * GLA_DV, LANES), 1) >= GLA_DK
    st_mask = st_r == st_c
    norm = norm_ref[...]

    work = []
    for ci in range(tl // c):
        r0 = ci * c
        gcum = _dot01_left(tril01, loga_ref[r0:r0 + c, :])
        q = qk_ref[r0:r0 + c, 0:GLA_QK].astype(F32)
        k = qk_ref[r0:r0 + c, GLA_QK:2 * GLA_QK].astype(F32)
        g_end = gcum[c - 1:c, :]
        refs = [jnp.zeros((1, GLA_QK), F32)] + [gcum[GLA_SUB * i - 1:GLA_SUB * i, :] for i in range(1, nsub)]
        ref_rows = jnp.concatenate([jnp.broadcast_to(r, (GLA_SUB, GLA_QK)) for r in refs], axis=0)
        qn = q * jnp.exp(gcum - ref_rows)
        qdec = q * jnp.exp(gcum)
        kdec = k * jnp.exp(g_end - gcum)
        for p in range(GLA_HEADS // 2):
            ps = slice(p * LANES, (p + 1) * LANES)
            kp = k[:, ps]
            gp = gcum[:, ps]
            sc = [[], []]
            for i in range(nsub):
                e = jnp.where(rows < GLA_SUB * (i + 1), refs[i][:, ps] - gp, 0.0)
                kn = (kp * jnp.exp(e)).astype(BF16)
                qi = qn[GLA_SUB * i:GLA_SUB * (i + 1), ps]
                for hh in range(2):
                    lhs = jnp.where((lane >= GLA_DK) if hh else (lane < GLA_DK), qi, 0.0)
                    s = _dot_nt(lhs, kn)
                    sc[hh].append(jnp.where(scol <= srow + GLA_SUB * i, s, 0.0))
            vp = v_ref[r0:r0 + c, p * 2 * GLA_DV:(p + 1) * 2 * GLA_DV]
            upd = jnp.where(st_mask, _dot_tn(vp, kdec[:, ps]), 0.0)
            intra = [_dot(jnp.concatenate(sc[hh], axis=0),
                          v_ref[r0:r0 + c, (2 * p + hh) * GLA_DV:(2 * p + hh + 1) * GLA_DV])
                     for hh in range(2)]
            work.append((r0, p, qdec[:, ps].astype(BF16), upd, jnp.exp(g_end[:, ps]), intra))

    for r0, p, qd, upd, dec, intra in work:
        st = st_ref[p]
        o_inter = _dot_nt(qd, st)
        st_ref[p] = st * dec + upd
        for hh in range(2):
            hs = slice((2 * p + hh) * GLA_DV, (2 * p + hh + 1) * GLA_DV)
            o = intra[hh] + o_inter[:, hh * GLA_DV:(hh + 1) * GLA_DV]
            o = o * lax.rsqrt(jnp.mean(o * o, axis=-1, keepdims=True) + LN_EPS) * norm
            out_ref[r0:r0 + c, hs] = (o * g_ref[r0:r0 + c, hs].astype(F32)).astype(BF16)


def _gla_call(layer, bsz, seqlen, gqk, gv, gg, loga, pp):
    tl = SEQ_TILE
    nl = seqlen // tl
    row = lambda n: pl.BlockSpec((tl, n), lambda b, l: (b * nl + l, 0))
    return pl.pallas_call(
        _gla_kernel,
        grid=(bsz, nl),
        in_specs=[row(2 * GLA_QK), row(GLA_VW), row(GLA_VW), row(GLA_QK),
                  _const_spec((1, LANES), layer)],
        out_specs=row(GLA_VW),
        out_shape=jax.ShapeDtypeStruct((bsz * seqlen, GLA_VW), BF16),
        scratch_shapes=[pltpu.VMEM((GLA_HEADS // 2, 2 * GLA_DV, 2 * GLA_DK), F32)],
        compiler_params=_params(("arbitrary", "arbitrary")),
        name="gla",
    )(gqk, gv, gg, loga, pp["gla_norm"])


S5_GPT = LANES // S5_GROUP


def _s5_kernel(u_ref, kc_ref, ptr_ref, pti_ref, qr_ref, qi_ref, ac_ref, y_ref,
               xs_ref, ug_ref, yg_ref, ys_ref, t_ref):
    c = S5_CHUNK
    nch = u_ref.shape[0] // c
    xs_ref[...] = u_ref[...].astype(F32)
    lane = lax.broadcasted_iota(jnp.int32, (nch, LANES), 1)
    piece = [(lane >= S5_GROUP * i) & (lane < S5_GROUP * (i + 1)) for i in range(S5_GPT)]

    for q in range(c // S5_GPT):
        tiles = [None] * S5_GPT
        for i in range(S5_GPT):
            r = xs_ref[pl.ds(S5_GPT * q + i, nch, stride=c), :]
            for g in range(S5_GPT):
                sh = (S5_GROUP * (i - g)) % LANES
                rr = pltpu.roll(r, sh, axis=1) if sh else r
                tiles[g] = jnp.where(piece[i], rr, 0.0 if tiles[g] is None else tiles[g])
        for g in range(S5_GPT):
            ug_ref[g, :, LANES * q:LANES * (q + 1)] = tiles[g].astype(BF16)

    klane = lax.broadcasted_iota(jnp.int32, (S5_GROUP, c * S5_GROUP), 1)
    clane = lax.broadcasted_iota(jnp.int32, (S5_STATE, nch), 1)
    xrs, xis = [], []
    for g in range(S5_GPT):
        kc = kc_ref[g]
        for s in range(c):
            blk = kc if s == 0 else jnp.where(klane >= S5_GROUP * s,
                                              pltpu.roll(kc, S5_GROUP * s, axis=1), 0.0)
            t_ref[g, S5_GROUP * s:S5_GROUP * (s + 1), :] = blk.astype(BF16)
        u = ug_ref[g]
        yg_ref[g] = jnp.dot(u, t_ref[g], preferred_element_type=F32)
        xrs.append(_dot_nt(ptr_ref[g], u))
        xis.append(_dot_nt(pti_ref[g], u))
    prs = [ac_ref[g][:, 0:1] for g in range(S5_GPT)]
    pis = [ac_ref[g][:, 1:2] for g in range(S5_GPT)]
    dist = 1
    while dist < nch:
        for g in range(S5_GPT):
            sr = jnp.where(clane >= dist, pltpu.roll(xrs[g], dist, axis=1), 0.0)
            si = jnp.where(clane >= dist, pltpu.roll(xis[g], dist, axis=1), 0.0)
            pr, pi = prs[g], pis[g]
            xrs[g], xis[g] = xrs[g] + pr * sr - pi * si, xis[g] + pr * si + pi * sr
            prs[g], pis[g] = pr * pr - pi * pi, 2.0 * pr * pi
        dist *= 2
    for g in range(S5_GPT):
        xr = jnp.where(clane >= 1, pltpu.roll(xrs[g], 1, axis=1), 0.0)
        xi = jnp.where(clane >= 1, pltpu.roll(xis[g], 1, axis=1), 0.0)
        yg_ref[g] = yg_ref[g] + _dot_tn(xr, qr_ref[g]) + _dot_tn(xi, qi_ref[g])

    for s in range(c):
        q, i = divmod(s, S5_GPT)
        tile = None
        for g in range(S5_GPT):
            r = yg_ref[g, :, LANES * q:LANES * (q + 1)]
            sh = (S5_GROUP * (g - i)) % LANES
            if sh:
                r = pltpu.roll(r, sh, axis=1)
            tile = jnp.where(piece[g], r, 0.0 if tile is None else tile)
        ys_ref[pl.ds(s, nch, stride=c), :] = tile
    y_ref[...] = ys_ref[...].astype(BF16)


def _s5_call(layer, bsz, u2, pp):
    c = S5_CHUNK
    seqlen = u2.shape[0] // bsz
    nch = seqlen // c
    assert nch <= LANES, "the chunk scan keeps one sequence's chunks inside one lane tile"
    tab = lambda a, b: pl.BlockSpec((None, S5_GPT, a, b), lambda bi, j: (layer, j, 0, 0))
    io = pl.BlockSpec((seqlen, LANES), lambda bi, j: (bi, j))
    return pl.pallas_call(
        _s5_kernel,
        grid=(bsz, u2.shape[1] // LANES),
        in_specs=[io, tab(S5_GROUP, c * S5_GROUP), tab(S5_STATE, c * S5_GROUP),
                  tab(S5_STATE, c * S5_GROUP), tab(S5_STATE, c * S5_GROUP),
                  tab(S5_STATE, c * S5_GROUP), tab(S5_STATE, LANES)],
        out_specs=io,
        out_shape=jax.ShapeDtypeStruct(u2.shape, BF16),
        scratch_shapes=[pltpu.VMEM((seqlen, LANES), F32),
                        pltpu.VMEM((S5_GPT, nch, c * S5_GROUP), BF16),
                        pltpu.VMEM((S5_GPT, nch, c * S5_GROUP), F32),
                        pltpu.VMEM((seqlen, LANES), F32),
                        pltpu.VMEM((S5_GPT, c * S5_GROUP, c * S5_GROUP), BF16)],
        compiler_params=_params(("arbitrary", "arbitrary")),
        name="s5",
    )(u2, pp["s5_kc"], pp["s5_ptr"], pp["s5_pti"], pp["s5_qr"], pp["s5_qi"], pp["s5_ac"])


def _merge_kernel(x_ref, a_ref, cfo_ref, z_ref, d_ref, gates_ref,
                  wdn_ref, wcf_ref, ws5_ref, wgla_ref, wo_ref, lng_ref, lnb_ref, out_ref):
    gt = lambda i: gates_ref[:, i * D_MODEL:(i + 1) * D_MODEL].astype(F32)
    y_a = jnp.dot(a_ref[...], wdn_ref[...], preferred_element_type=F32)
    y_b = jnp.dot(cfo_ref[...], wcf_ref[...], preferred_element_type=F32)
    y_d = jnp.dot(d_ref[...], wgla_ref[...], preferred_element_type=F32)
    zg = _gelu_tanh(z_ref[...].astype(F32)).astype(BF16)
    z_val = jnp.dot(zg, ws5_ref[:, 0:D_MODEL], preferred_element_type=F32)
    z_gate = jnp.dot(zg, ws5_ref[:, D_MODEL:2 * D_MODEL], preferred_element_type=F32)
    y_c = z_val * _sigmoid(z_gate)
    merged = gt(0) * y_a + gt(1) * y_b + gt(2) * y_c + gt(3) * y_d
    mix = jnp.dot(merged.astype(BF16), wo_ref[...], preferred_element_type=F32)
    out_ref[...] = _layer_norm(DEEPNORM_ALPHA * x_ref[...] + mix, lng_ref[...], lnb_ref[...])


def _merge_call(layer, bsz, seqlen, x2, a_in, cf, z, d_in, gates, pp):
    tl = MERGE_TILE
    nl = seqlen // tl
    row = lambda n: pl.BlockSpec((tl, n), lambda b, l: (b * nl + l, 0))
    return pl.pallas_call(
        _merge_kernel,
        grid=(bsz, nl),
        in_specs=[row(D_MODEL), row(512), row(CF_WIDTH), row(S5_WIDTH), row(GLA_VW),
                  row(N_BRANCH * D_MODEL),
                  _const_spec((512, D_MODEL), layer), _const_spec((CF_WIDTH, D_MODEL), layer),
                  _const_spec((S5_WIDTH, 2 * D_MODEL), layer), _const_spec((GLA_VW, D_MODEL), layer),
                  _const_spec((D_MODEL, D_MODEL), layer),
                  _const_spec((1, D_MODEL), layer), _const_spec((1, D_MODEL), layer)],
        out_specs=row(D_MODEL),
        out_shape=jax.ShapeDtypeStruct((bsz * seqlen, D_MODEL), F32),
        compiler_params=_params(("arbitrary", "arbitrary")),
        name="merge",
    )(x2, a_in, cf, z, d_in, gates, pp["w_br_dn"], pp["w_br_cf"], pp["w_br_s5"], pp["w_br_gla"],
      pp["w_o"], pp["ln1_g"], pp["ln1_b"])


def _ffn_kernel(x_ref, wup_ref, convw_ref, wdown_ref, lng_ref, lnb_ref, out_ref, ext_ref, hid_ref):
    tl = x_ref.shape[0]
    w = FFN_COLS

    @pl.when(pl.program_id(1) == 0)
    def _():
        ext_ref[0:SUBLANES, :] = jnp.zeros((SUBLANES, 2 * D_FF), F32)

    x = x_ref[...]
    xb = x.astype(BF16)
    nchunks = D_FF // w
    cols = lambda ci, half: slice(half * D_FF + ci * w, half * D_FF + (ci + 1) * w)

    def up(ci):
        us = []
        for half in range(2):
            u = jnp.dot(xb, wup_ref[:, cols(ci, half)], preferred_element_type=F32)
            ext_ref[SUBLANES:SUBLANES + tl, cols(ci, half)] = u
            us.append(u)
        return us

    us = up(0)
    for ci in range(nchunks):
        nxt = up(ci + 1) if ci + 1 < nchunks else None
        halves = []
        for half in range(2):
            cs = cols(ci, half)
            y = convw_ref[FFN_CONV - 1:FFN_CONV, cs] * us[half]
            for k in range(FFN_CONV - 1):
                off = SUBLANES - (FFN_CONV - 1) + k
                y = y + convw_ref[k:k + 1, cs] * ext_ref[off:off + tl, cs]
            halves.append(y)
        hid_ref[:, ci * w:(ci + 1) * w] = (_silu(halves[0]) * halves[1]).astype(BF16)
        us = nxt
    acc = jnp.dot(hid_ref[...], wdown_ref[...], preferred_element_type=F32)
    ext_ref[0:SUBLANES, :] = ext_ref[tl:tl + SUBLANES, :]
    out_ref[...] = _layer_norm(DEEPNORM_ALPHA * x + acc, lng_ref[...], lnb_ref[...])


def _ffn_call(layer, bsz, seqlen, x2, pp):
    tl = FFN_TILE
    nl = seqlen // tl
    row = lambda n: pl.BlockSpec((tl, n), lambda b, l: (b * nl + l, 0))
    return pl.pallas_call(
        _ffn_kernel,
        grid=(bsz, nl),
        in_specs=[row(D_MODEL),
                  _const_spec((D_MODEL, 2 * D_FF), layer), _const_spec((SUBLANES, 2 * D_FF), layer),
                  _const_spec((D_FF, D_MODEL), layer),
                  _const_spec((1, D_MODEL), layer), _const_spec((1, D_MODEL), layer)],
        out_specs=row(D_MODEL),
        out_shape=jax.ShapeDtypeStruct((bsz * seqlen, D_MODEL), F32),
        scratch_shapes=[pltpu.VMEM((tl + SUBLANES, 2 * D_FF), F32),
                        pltpu.VMEM((tl, D_FF), BF16)],
        compiler_params=_params(("arbitrary", "arbitrary")),
        name="convffn",
    )(x2, pp["w_up"], pp["ffn_conv"], pp["w_down"], pp["ln2_g"], pp["ln2_b"])


def _pad_rows(a, rows):
    return jnp.pad(a, ((0, 0), (0, rows - a.shape[1]), (0, 0)))


def _s5_tables(a_re, a_im, log_dt, b_re, b_im, c_re, c_im, d):
    c = S5_CHUNK
    dt = jnp.exp(log_dt)[..., None]
    mag = jnp.exp(dt * a_re)
    abar_re, abar_im = mag * jnp.cos(dt * a_im), mag * jnp.sin(dt * a_im)
    den = a_re * a_re + a_im * a_im
    nr, ni = abar_re - 1.0, abar_im
    fr, fi = (nr * a_re + ni * a_im) / den, (ni * a_re - nr * a_im) / den
    bb_re = fr[..., None] * b_re - fi[..., None] * b_im
    bb_im = fr[..., None] * b_im + fi[..., None] * b_re
    j = jnp.arange(c + 1, dtype=F32)[:, None, None, None]
    pmag = jnp.exp(j * (dt * a_re)[None])
    pw_re = pmag * jnp.cos(j * (dt * a_im)[None])
    pw_im = pmag * jnp.sin(j * (dt * a_im)[None])
    cb_re = jnp.einsum('lgon,lgni->lgnoi', c_re, bb_re) - jnp.einsum('lgon,lgni->lgnoi', c_im, bb_im)
    cb_im = jnp.einsum('lgon,lgni->lgnoi', c_re, bb_im) + jnp.einsum('lgon,lgni->lgnoi', c_im, bb_re)
    kern = (jnp.einsum('jlgn,lgnoi->jlgoi', pw_re[:c], cb_re)
            - jnp.einsum('jlgn,lgnoi->jlgoi', pw_im[:c], cb_im))
    eye = jnp.eye(S5_GROUP, dtype=F32)
    kern = kern.at[0].add(d.reshape(d.shape[0], S5_GROUPS, S5_GROUP)[..., None] * eye)
    kcat = kern.transpose(1, 2, 4, 0, 3).reshape(d.shape[0], S5_GROUPS, S5_GROUP, c * S5_GROUP)
    rp_re, rp_im = pw_re[:c][::-1], pw_im[:c][::-1]
    p_re = rp_re[..., None] * bb_re[None] - rp_im[..., None] * bb_im[None]
    p_im = rp_re[..., None] * bb_im[None] + rp_im[..., None] * bb_re[None]
    tos = lambda a: a.transpose(1, 2, 3, 0, 4).reshape(d.shape[0], S5_GROUPS, S5_STATE, c * S5_GROUP)
    q_re = (jnp.einsum('lgon,tlgn->lgnto', c_re, pw_re[1:]) - jnp.einsum('lgon,tlgn->lgnto', c_im, pw_im[1:]))
    q_im = -(jnp.einsum('lgon,tlgn->lgnto', c_re, pw_im[1:]) + jnp.einsum('lgon,tlgn->lgnto', c_im, pw_re[1:]))
    toq = lambda a: a.reshape(d.shape[0], S5_GROUPS, S5_STATE, c * S5_GROUP)
    ac = jnp.stack([pw_re[c], pw_im[c]], axis=-1)
    ac = jnp.pad(ac, ((0, 0), (0, 0), (0, 0), (0, LANES - 2)))
    return dict(s5_kc=kcat, s5_ptr=tos(p_re).astype(BF16), s5_pti=tos(p_im).astype(BF16),
                s5_qr=toq(q_re).astype(BF16), s5_qi=toq(q_im).astype(BF16), s5_ac=ac)


def _pack_params(w_in, dn_conv, dn_a_log, dn_dt_bias, dn_norm, w_br_dn, cf_dw, cf_dw_bias, cf_ln_g,
                 cf_ln_b, w_br_cf, s5_a_re, s5_a_im, s5_log_dt, s5_b_re, s5_b_im, s5_c_re, s5_c_im,
                 s5_d, w_br_s5, gla_w_alpha, gla_b_alpha, gla_norm, w_br_gla, w_o, ln1_g, ln1_b,
                 w_up, ffn_conv, w_down, ln2_g, ln2_b):
    sizes = (DN_QKV, DN_HEADS, DN_HEADS, 512, 2 * CF_WIDTH, S5_WIDTH, GLA_QK, GLA_QK, GLA_VW, GLA_VW,
             GLA_RANK, N_BRANCH * D_MODEL)
    offs = [0]
    for s in sizes:
        offs.append(offs[-1] + s)
    col = lambda i: w_in[:, :, offs[i]:offs[i + 1]]
    w_small = jnp.concatenate([col(1), col(2), col(10)], axis=-1)
    w_small = jnp.pad(w_small, ((0, 0), (0, 0), (0, LANES - w_small.shape[-1])))
    ws_hi = w_small.astype(BF16)
    ws_lo = (w_small - ws_hi.astype(F32)).astype(BF16)
    pad_l = lambda a: jnp.pad(a, ((0, 0), (0, LANES - a.shape[-1])))
    dn_vec = jnp.stack([pad_l(-jnp.exp(dn_a_log)), pad_l(dn_dt_bias)], axis=1)
    dn_vec = _pad_rows(dn_vec, SUBLANES)
    wal = jnp.pad(gla_w_alpha, ((0, 0), (2 * DN_HEADS, LANES - 2 * DN_HEADS - GLA_RANK), (0, 0)))
    wal_hi = wal.astype(BF16)
    wal_lo = (wal - wal_hi.astype(F32)).astype(BF16)
    vec = lambda a: a[:, None, :]
    pp = dict(
        w_qkv=col(0).astype(BF16), w_mid=w_in[:, :, offs[3]:offs[10]].astype(BF16),
        w_gate=col(11).astype(BF16), ws_hi=ws_hi, ws_lo=ws_lo, dn_vec=dn_vec, wal_hi=wal_hi, wal_lo=wal_lo,
        b_alpha=vec(gla_b_alpha),
        dn_conv=_pad_rows(dn_conv, SUBLANES), dn_norm=vec(dn_norm), gla_norm=vec(gla_norm),
        w_br_dn=w_br_dn.astype(BF16), w_br_cf=w_br_cf.astype(BF16), w_br_s5=w_br_s5.astype(BF16),
        w_br_gla=w_br_gla.astype(BF16), w_o=w_o.astype(BF16),
        cf_dw=_pad_rows(cf_dw, CF_HALO), cf_dw_bias=vec(cf_dw_bias), cf_ln_g=vec(cf_ln_g),
        cf_ln_b=vec(cf_ln_b), ln1_g=vec(ln1_g), ln1_b=vec(ln1_b),
        w_up=w_up.astype(BF16), ffn_conv=_pad_rows(ffn_conv, SUBLANES), w_down=w_down.astype(BF16),
        ln2_g=vec(ln2_g), ln2_b=vec(ln2_b),
    )
    pp.update(_s5_tables(s5_a_re, s5_a_im, s5_log_dt, s5_b_re, s5_b_im, s5_c_re, s5_c_im, s5_d))
    return pp


def kernel(x, w_in, dn_conv, dn_a_log, dn_dt_bias, dn_norm, w_br_dn, cf_dw, cf_dw_bias, cf_ln_g, cf_ln_b, w_br_cf, s5_a_re, s5_a_im, s5_log_dt, s5_b_re, s5_b_im, s5_c_re, s5_c_im, s5_d, w_br_s5, gla_w_alpha, gla_b_alpha, gla_norm, w_br_gla, w_o, ln1_g, ln1_b, w_up, ffn_conv, w_down, ln2_g, ln2_b):
    bsz, seqlen, d_model = x.shape
    assert d_model == D_MODEL and bsz == SUBLANES
    assert all(seqlen % t == 0 for t in (DN_TILE, SEQ_TILE, MERGE_TILE, FFN_TILE, S5_CHUNK))
    assert seqlen % PROJ_TILE == 0
    depth = w_in.shape[0]
    pp = _pack_params(w_in, dn_conv, dn_a_log, dn_dt_bias, dn_norm, w_br_dn, cf_dw, cf_dw_bias,
                      cf_ln_g, cf_ln_b, w_br_cf, s5_a_re, s5_a_im, s5_log_dt, s5_b_re, s5_b_im,
                      s5_c_re, s5_c_im, s5_d, w_br_s5, gla_w_alpha, gla_b_alpha, gla_norm, w_br_gla,
                      w_o, ln1_g, ln1_b, w_up, ffn_conv, w_down, ln2_g, ln2_b)
    t = bsz * seqlen
    x2 = x.reshape(t, D_MODEL)
    for layer in range(depth):
        (qkv, dng, cf, s5_in, gqk, gv, gg, gates, small, loga) = _proj_call(layer, seqlen, x2, pp)
        smallt = small[:, :SUBLANES].reshape(t // DN_CHUNK, DN_CHUNK, SUBLANES).transpose(0, 2, 1)
        a_in = _dn_call(layer, bsz, seqlen, qkv, small, smallt, dng, pp)
        d_in = _gla_call(layer, bsz, seqlen, gqk, gv, gg, loga, pp)
        z = _s5_call(layer, bsz, s5_in, pp)
        x2 = _merge_call(layer, bsz, seqlen, x2, a_in, cf, z, d_in, gates, pp)
        x2 = _ffn_call(layer, bsz, seqlen, x2, pp)
    return x2.reshape(bsz, seqlen, D_MODEL)
```
